```python
import math
import jax, jax.numpy as jnp
from jax import lax
import numpy as np

D_MODEL = 1024
BATCH = 8
SEQ = 2048
DEPTH = 1
DEC_BATCH = 8
DEC_SEQ = 8192
PAST_LEN = 128

SSD_WIDTH = D_MODEL
SSD_HEADDIM = 64
SSD_HEADS = SSD_WIDTH // SSD_HEADDIM
SSD_GROUPS = 2
SSD_HPG = SSD_HEADS // SSD_GROUPS
SSD_STATE = 128
SSD_CHUNK = 128
CONV_K = 5
CONV_CH = SSD_WIDTH + 2 * SSD_GROUPS * SSD_STATE
GLA_HEADS = 4
GLA_KEY = D_MODEL // 2
GLA_VAL = D_MODEL
GLA_DK = GLA_KEY // GLA_HEADS
GLA_DV = GLA_VAL // GLA_HEADS
GLA_LOWRANK = 16
GLA_GATE_NORM = 16.0
GLA_CHUNK = 64
MIX_WIDTH = SSD_WIDTH + GLA_VAL
IN_SIZES = (SSD_WIDTH, SSD_WIDTH, SSD_GROUPS * SSD_STATE, SSD_GROUPS * SSD_STATE, SSD_HEADS, SSD_HEADS,
            GLA_KEY, GLA_KEY, GLA_VAL, GLA_VAL, GLA_LOWRANK)
IN_PROJ = sum(IN_SIZES)
MEM_TOKENS = 256
MEM_HEADS = 4
MEM_HD = D_MODEL // MEM_HEADS
D_FF = 4 * D_MODEL
ALPHA = (2 * DEPTH) ** 0.25
BETA = (8 * DEPTH) ** -0.25
LN_EPS = 1e-5
RMS_EPS = 1e-5

kernel_name = "hybrid_ssd_gla_memory_encoder"


def split_cols(a, sizes):
    offs = [0]
    for s in sizes:
        offs.append(offs[-1] + s)
    return [a[..., offs[i]:offs[i + 1]] for i in range(len(sizes))]


def layer_norm(x, g, b):
    xf = x.astype(jnp.float32)
    mu = jnp.mean(xf, -1, keepdims=True)
    var = jnp.mean(jnp.square(xf - mu), -1, keepdims=True)
    return ((xf - mu) * lax.rsqrt(var + LN_EPS) * g + b).astype(x.dtype)


def rms_norm(x, g):
    xf = x.astype(jnp.float32)
    return xf * lax.rsqrt(jnp.mean(jnp.square(xf), -1, keepdims=True) + RMS_EPS) * g


def centred_dwconv(u, w, b):
    out = lax.conv_general_dilated(u, w[:, None, :], window_strides=(1,),
                                   padding=[((CONV_K - 1) // 2, CONV_K // 2)],
                                   dimension_numbers=('NWC', 'WIO', 'NWC'),
                                   feature_group_count=u.shape[-1])
    return out + b


def ssd_chunked(x, dt, a_neg, bm, cm):
    b, L = x.shape[:2]
    Q = SSD_CHUNK
    nc = L // Q
    xc = x.reshape(b, nc, Q, SSD_GROUPS, SSD_HPG, SSD_HEADDIM)
    dtc = dt.reshape(b, nc, Q, SSD_GROUPS, SSD_HPG)
    bc = bm.reshape(b, nc, Q, SSD_GROUPS, SSD_STATE)
    cc = cm.reshape(b, nc, Q, SSD_GROUPS, SSD_STATE)
    acs = jnp.cumsum(dtc * a_neg.reshape(SSD_GROUPS, SSD_HPG), axis=2)
    causal = jnp.tril(jnp.ones((Q, Q), bool))[:, :, None, None]
    seg = acs[:, :, :, None] - acs[:, :, None, :]
    lmat = jnp.exp(jnp.where(causal, seg, -jnp.inf))
    xdt = xc * dtc[..., None]
    cb = jnp.einsum('bclgn,bcsgn->bclsg', cc, bc)
    y_diag = jnp.einsum('bclsg,bclsgh,bcsghp->bclghp', cb, lmat, xdt)
    decay_to_end = jnp.exp(acs[:, :, -1:] - acs)
    chunk_states = jnp.einsum('bcsgn,bcsgh,bcsghp->bcghpn', bc, decay_to_end, xdt)
    chunk_decay = jnp.exp(acs[:, :, -1])

    def step(h, inp):
        st, dec = inp
        return h * dec[..., None, None] + st, h

    h0 = jnp.zeros((b, SSD_GROUPS, SSD_HPG, SSD_HEADDIM, SSD_STATE), jnp.float32)
    _, h_in = lax.scan(step, h0, (jnp.moveaxis(chunk_states, 1, 0), jnp.moveaxis(chunk_decay, 1, 0)))
    h_in = jnp.moveaxis(h_in, 0, 1)
    y_off = jnp.einsum('bclgn,bcghpn,bclgh->bclghp', cc, h_in, jnp.exp(acs))
    return (y_diag + y_off).reshape(b, L, SSD_HEADS, SSD_HEADDIM)


def gla_chunked(q, k, v, lg):
    b, L, H, K = q.shape
    V = v.shape[-1]
    Q = GLA_CHUNK
    nc = L // Q
    qc = q.reshape(b, nc, Q, H, K)
    kc = k.reshape(b, nc, Q, H, K)
    vc = v.reshape(b, nc, Q, H, V)
    bcs = jnp.cumsum(lg.reshape(b, nc, Q, H, K), axis=2)
    q_t = qc * jnp.exp(bcs)
    k_t = kc * jnp.exp(-bcs)
    incl = jnp.tril(jnp.ones((Q, Q), bool))
    att = jnp.where(incl, jnp.einsum('bclhk,bcshk->bchls', q_t, k_t), 0.0)
    o_intra = jnp.einsum('bchls,bcshv->bclhv', att, vc)
    k_end = kc * jnp.exp(bcs[:, :, -1:] - bcs)
    chunk_states = jnp.einsum('bcshk,bcshv->bchkv', k_end, vc)
    chunk_decay = jnp.exp(bcs[:, :, -1])

    def step(h, inp):
        st, dec = inp
        return h * dec[..., None] + st, h

    h0 = jnp.zeros((b, H, K, V), jnp.float32)
    _, h_in = lax.scan(step, h0, (jnp.moveaxis(chunk_states, 1, 0), jnp.moveaxis(chunk_decay, 1, 0)))
    h_in = jnp.moveaxis(h_in, 0, 1)
    o_inter = jnp.einsum('bclhk,bchkv->bclhv', q_t, h_in)
    return (o_intra + o_inter).reshape(b, L, H, V)


def flip(a):
    return jnp.flip(a, axis=1)


def hybrid_mixer(x, w_in, conv_w, conv_b, a_log_f, a_log_b, dt_bias_f, dt_bias_b, d_skip, ssd_norm_g,
                 w_gk_f, b_gk_f, w_gk_b, b_gk_b, gla_norm_g, w_out):
    b, L, _ = x.shape
    f32 = jnp.float32
    z, xs, bm, cm, dt_f, dt_b, q, k, v, g, gk_lr = split_cols(x @ w_in, IN_SIZES)
    xbc = jax.nn.silu(centred_dwconv(jnp.concatenate([xs, bm, cm], -1), conv_w, conv_b))
    xs, bm, cm = split_cols(xbc, (SSD_WIDTH, SSD_GROUPS * SSD_STATE, SSD_GROUPS * SSD_STATE))
    xh = xs.reshape(b, L, SSD_HEADS, SSD_HEADDIM).astype(f32)
    bm = bm.reshape(b, L, SSD_GROUPS, SSD_STATE).astype(f32)
    cm = cm.reshape(b, L, SSD_GROUPS, SSD_STATE).astype(f32)
    dtf = jax.nn.softplus(dt_f.astype(f32) + dt_bias_f.astype(f32))
    dtb = jax.nn.softplus(dt_b.astype(f32) + dt_bias_b.astype(f32))
    a_f = -jnp.exp(a_log_f.astype(f32))
    a_b = -jnp.exp(a_log_b.astype(f32))
    y = (ssd_chunked(xh, dtf, a_f, bm, cm)
         + flip(ssd_chunked(flip(xh), flip(dtb), a_b, flip(bm), flip(cm)))
         + d_skip.astype(f32)[:, None] * xh)
    y = y.reshape(b, L, SSD_WIDTH) * jax.nn.silu(z.astype(f32))
    y = rms_norm(y.reshape(b, L, SSD_GROUPS, SSD_WIDTH // SSD_GROUPS),
                 ssd_norm_g.reshape(SSD_GROUPS, -1)).reshape(b, L, SSD_WIDTH)
    qh = q.reshape(b, L, GLA_HEADS, GLA_DK).astype(f32) * (GLA_DK ** -0.5)
    kh = k.reshape(b, L, GLA_HEADS, GLA_DK).astype(f32)
    vh = v.reshape(b, L, GLA_HEADS, GLA_DV).astype(f32)
    lr = gk_lr.astype(f32)
    lg_f = (jax.nn.log_sigmoid(lr @ w_gk_f.astype(f32) + b_gk_f) / GLA_GATE_NORM).reshape(b, L, GLA_HEADS, GLA_DK)
    lg_b = (jax.nn.log_sigmoid(lr @ w_gk_b.astype(f32) + b_gk_b) / GLA_GATE_NORM).reshape(b, L, GLA_HEADS, GLA_DK)
    o = gla_chunked(qh, kh, vh, lg_f) + flip(gla_chunked(flip(qh), flip(kh), flip(vh), flip(lg_b)))
    o = rms_norm(o, gla_norm_g) * jax.nn.silu(g.reshape(b, L, GLA_HEADS, GLA_DV).astype(f32))
    mix = jnp.concatenate([y, o.reshape(b, L, GLA_VAL)], -1).astype(x.dtype)
    return mix @ w_out


def memory_cross_attention(x, mem, w_mq, w_mk, w_mv, w_mo):
    b, L, _ = x.shape
    m = mem.shape[1]
    q = (x @ w_mq).reshape(b, L, MEM_HEADS, MEM_HD)
    k = (mem @ w_mk).reshape(b, m, MEM_HEADS, MEM_HD)
    v = (mem @ w_mv).reshape(b, m, MEM_HEADS, MEM_HD)
    s = jnp.einsum('blhd,bmhd->bhlm', q, k).astype(jnp.float32) * (MEM_HD ** -0.5)
    p = jax.nn.softmax(s, axis=-1).astype(v.dtype)
    o = jnp.einsum('bhlm,bmhd->blhd', p, v).reshape(b, L, D_MODEL)
    return o @ w_mo


def squared_relu_mlp(x, w_ff1, w_ff2):
    return jnp.square(jax.nn.relu(x @ w_ff1)) @ w_ff2


def encoder_layer(x, mem, w_in, conv_w, conv_b, a_log_f, a_log_b, dt_bias_f, dt_bias_b, d_skip, ssd_norm_g,
                  w_gk_f, b_gk_f, w_gk_b, b_gk_b, gla_norm_g, w_out, ln1_g, ln1_b,
                  w_mq, w_mk, w_mv, w_mo, ln2_g, ln2_b, w_ff1, w_ff2, ln3_g, ln3_b):
    x = layer_norm(ALPHA * x + hybrid_mixer(x, w_in, conv_w, conv_b, a_log_f, a_log_b, dt_bias_f, dt_bias_b,
                                             d_skip, ssd_norm_g, w_gk_f, b_gk_f, w_gk_b, b_gk_b,
                                             gla_norm_g, w_out), ln1_g, ln1_b)
    x = layer_norm(ALPHA * x + memory_cross_attention(x, mem, w_mq, w_mk, w_mv, w_mo), ln2_g, ln2_b)
    x = layer_norm(ALPHA * x + squared_relu_mlp(x, w_ff1, w_ff2), ln3_g, ln3_b)
    return x


def run_trunk(x, mem, params):
    for i in range(DEPTH):
        x = encoder_layer(x, mem, *[p[i] for p in params])
    return x


def setup_inputs(seed: int = 0) -> dict:
    key = jax.random.key(seed)
    ks = jax.random.split(key, 40)
    f32 = jnp.float32
    nrm = lambda k, shape, scale: jax.random.normal(k, shape, f32) * scale
    dt0 = jnp.exp(jax.random.uniform(ks[8], (DEPTH, SSD_HEADS), f32, math.log(1e-3), math.log(1e-1)))
    dt1 = jnp.exp(jax.random.uniform(ks[9], (DEPTH, SSD_HEADS), f32, math.log(1e-3), math.log(1e-1)))
    inv_softplus = lambda d: d + jnp.log(-jnp.expm1(-d))
    return {
        "x_prompt": nrm(ks[0], (BATCH, SEQ, D_MODEL), 1.0),
        "x_sample": nrm(ks[1], (DEC_BATCH, DEC_SEQ, D_MODEL), 1.0),
        "mem_prompt": nrm(ks[2], (BATCH, MEM_TOKENS, D_MODEL), 1.0),
        "mem_sample": nrm(ks[3], (DEC_BATCH, MEM_TOKENS, D_MODEL), 1.0),
        "w_in": nrm(ks[4], (DEPTH, D_MODEL, IN_PROJ), D_MODEL ** -0.5),
        "conv_w": nrm(ks[5], (DEPTH, CONV_K, CONV_CH), CONV_K ** -0.5),
        "conv_b": nrm(ks[6], (DEPTH, CONV_CH), 0.02),
        "a_log_f": jnp.log(jax.random.uniform(ks[7], (DEPTH, SSD_HEADS), f32, 1.0, 16.0)),
        "a_log_b": jnp.log(jax.random.uniform(ks[10], (DEPTH, SSD_HEADS), f32, 1.0, 16.0)),
        "dt_bias_f": inv_softplus(dt0),
        "dt_bias_b": inv_softplus(dt1),
        "d_skip": 1.0 + nrm(ks[11], (DEPTH, SSD_HEADS), 0.02),
        "ssd_norm_g": 1.0 + nrm(ks[12], (DEPTH, SSD_WIDTH), 0.02),
        "w_gk_f": nrm(ks[13], (DEPTH, GLA_LOWRANK, GLA_KEY), GLA_LOWRANK ** -0.5),
        "b_gk_f": nrm(ks[14], (DEPTH, GLA_KEY), 0.02),
        "w_gk_b": nrm(ks[15], (DEPTH, GLA_LOWRANK, GLA_KEY), GLA_LOWRANK ** -0.5),
        "b_gk_b": nrm(ks[16], (DEPTH, GLA_KEY), 0.02),
        "gla_norm_g": 1.0 + nrm(ks[17], (DEPTH, GLA_DV), 0.02),
        "w_out": nrm(ks[18], (DEPTH, MIX_WIDTH, D_MODEL), MIX_WIDTH ** -0.5 * BETA),
        "ln1_g": 1.0 + nrm(ks[19], (DEPTH, D_MODEL), 0.02),
        "ln1_b": nrm(ks[20], (DEPTH, D_MODEL), 0.02),
        "w_mq": nrm(ks[21], (DEPTH, D_MODEL, D_MODEL), D_MODEL ** -0.5),
        "w_mk": nrm(ks[22], (DEPTH, D_MODEL, D_MODEL), D_MODEL ** -0.5),
        "w_mv": nrm(ks[23], (DEPTH, D_MODEL, D_MODEL), D_MODEL ** -0.5 * BETA),
        "w_mo": nrm(ks[24], (DEPTH, D_MODEL, D_MODEL), D_MODEL ** -0.5 * BETA),
        "ln2_g": 1.0 + nrm(ks[25], (DEPTH, D_MODEL), 0.02),
        "ln2_b": nrm(ks[26], (DEPTH, D_MODEL), 0.02),
        "w_ff1": nrm(ks[27], (DEPTH, D_MODEL, D_FF), D_MODEL ** -0.5),
        "w_ff2": nrm(ks[28], (DEPTH, D_FF, D_MODEL), D_FF ** -0.5 * BETA),
        "ln3_g": 1.0 + nrm(ks[29], (DEPTH, D_MODEL), 0.02),
        "ln3_b": nrm(ks[30], (DEPTH, D_MODEL), 0.02),
    }


def reference(x_prompt, x_sample, mem_prompt, mem_sample, w_in, conv_w, conv_b, a_log_f, a_log_b,
              dt_bias_f, dt_bias_b, d_skip, ssd_norm_g, w_gk_f, b_gk_f, w_gk_b, b_gk_b, gla_norm_g, w_out,
              ln1_g, ln1_b, w_mq, w_mk, w_mv, w_mo, ln2_g, ln2_b, w_ff1, w_ff2, ln3_g, ln3_b):
    params = (w_in, conv_w, conv_b, a_log_f, a_log_b, dt_bias_f, dt_bias_b, d_skip, ssd_norm_g,
              w_gk_f, b_gk_f, w_gk_b, b_gk_b, gla_norm_g, w_out, ln1_g, ln1_b,
              w_mq, w_mk, w_mv, w_mo, ln2_g, ln2_b, w_ff1, w_ff2, ln3_g, ln3_b)
    y_prompt = run_trunk(x_prompt, mem_prompt, params)
    y_sample = run_trunk(x_sample, mem_sample, params)
    return (y_prompt, y_sample)
```

```python
import functools

import jax
import jax.numpy as jnp
from jax import lax
from jax.experimental import pallas as pl
from jax.experimental.pallas import tpu as pltpu

F32 = jnp.float32
BF16 = jnp.bfloat16

D_MODEL = 1024
SSD_WIDTH = 1024
SSD_HEADDIM = 64
SSD_HEADS = 16
SSD_GROUPS = 2
SSD_HPG = 8
SSD_STATE = 128
SSD_CHUNK = 128
SSD_GW = SSD_HPG * SSD_HEADDIM
CONV_K = 5
CONV_CH = SSD_WIDTH + 2 * SSD_GROUPS * SSD_STATE
GLA_HEADS = 4
GLA_KEY = 512
GLA_VAL = 1024
GLA_DK = 128
GLA_DV = 256
GLA_LOWRANK = 16
GLA_GATE_NORM = 16.0
GLA_CHUNK = 64
MIX_WIDTH = SSD_WIDTH + GLA_VAL
IN_SIZES = (SSD_WIDTH, SSD_WIDTH, SSD_GROUPS * SSD_STATE, SSD_GROUPS * SSD_STATE, SSD_HEADS, SSD_HEADS,
            GLA_KEY, GLA_KEY, GLA_VAL, GLA_VAL, GLA_LOWRANK)
MEM_HEADS = 4
MEM_HD = 256
D_FF = 4096
DEPTH = 1
ALPHA = (2 * DEPTH) ** 0.25
LN_EPS = 1e-5
RMS_EPS = 1e-5

LANES = 128
SUBLANES = 8
VMEM_LIMIT_BYTES = 56 * 1024 * 1024

SM_DTF = 0
SM_DTB = SSD_HEADS
SM_LR = 2 * SSD_HEADS
SM_USED_DT = 2 * SSD_HEADS

HALO = SUBLANES
CONV_L = (CONV_K - 1) // 2

PROJ_BLOCK = 256
SCAN_BLOCK = 256
ATTN_BLOCK = 256
FF_CHUNK = 1024


def _dot(a, b):
    return jnp.dot(a, b, preferred_element_type=F32)


def _dot_nt(a, b):
    return lax.dot_general(a, b, (((1,), (1,)), ((), ())), preferred_element_type=F32)


def _dot_tn(a, b):
    return lax.dot_general(a, b, (((0,), (0,)), ((), ())), preferred_element_type=F32)


def _sigmoid(x):
    return 1.0 / (1.0 + jnp.exp(-x))


def _silu(x):
    return x * _sigmoid(x)


def _softplus(x):
    return jnp.maximum(x, 0.0) + jnp.log1p(jnp.exp(-jnp.abs(x)))


def _split3(a):
    hi = a.astype(BF16)
    r = a - hi.astype(F32)
    mid = r.astype(BF16)
    lo = (r - mid.astype(F32)).astype(BF16)
    return hi, mid, lo


def _cumsum_rows(tri3, a):
    hi, mid, lo = _split3(a)
    return _dot(tri3, jnp.concatenate([hi, mid, lo], axis=0))


def _expand_heads(a, e2):
    hi = a.astype(BF16)
    lo = (a - hi.astype(F32)).astype(BF16)
    return _dot(jnp.concatenate([hi, lo], axis=1), e2)


def _layer_norm(h, g, b):
    mu = jnp.mean(h, axis=-1, keepdims=True)
    d = h - mu
    var = jnp.mean(d * d, axis=-1, keepdims=True)
    return d * lax.rsqrt(var + LN_EPS) * g + b


def _proj_kernel(x_ref, xp_ref, xn_ref, wz_ref, wxbc_ref, wq_ref, wk_ref, wv_ref, wg_ref, wsm_ref,
                 convw_ref, convb_ref, dtbias_ref,
                 z_ref, xbc_ref, q_ref, k_ref, v_ref, g_ref, sm_ref, ext_ref):
    i = pl.program_id(1)
    nb = pl.num_programs(1)
    tm = x_ref.shape[0]
    xb = x_ref[...].astype(BF16)
    z_ref[...] = _dot(xb, wz_ref[...])
    q_ref[...] = _dot(xb, wq_ref[...]) * (GLA_DK ** -0.5)
    k_ref[...] = _dot(xb, wk_ref[...])
    v_ref[...] = _dot(xb, wv_ref[...])
    g_ref[...] = _dot(xb, wg_ref[...])

    sm = _dot(xb, wsm_ref[...])
    lane = lax.broadcasted_iota(jnp.int32, sm.shape, 1)
    sm_ref[...] = jnp.where(lane < SM_USED_DT, _softplus(sm + dtbias_ref[...]), sm)

    xpb = xp_ref[...].astype(BF16)
    xnb = xn_ref[...].astype(BF16)
    has_prev = i > 0
    has_next = i < nb - 1
    cw = 512
    for c in range(0, CONV_CH, cw):
        w = wxbc_ref[:, c:c + cw]
        ext_ref[0:HALO, c:c + cw] = jnp.where(has_prev, _dot(xpb, w), 0.0)
        ext_ref[HALO:HALO + tm, c:c + cw] = _dot(xb, w)
        ext_ref[HALO + tm:HALO + tm + HALO, c:c + cw] = jnp.where(has_next, _dot(xnb, w), 0.0)
    for c in range(0, CONV_CH, cw):
        acc = jnp.broadcast_to(convb_ref[:, c:c + cw], (tm, cw))
        for j in range(CONV_K):
            r0 = HALO - CONV_L + j
            acc = acc + convw_ref[j:j + 1, c:c + cw] * ext_ref[r0:r0 + tm, c:c + cw]
        xbc_ref[:, c:c + cw] = _silu(acc)


def _proj_call(x, wz, wxbc, wq, wk, wv, wg, wsm, convw, convb, dtbias):
    B, L, _ = x.shape
    tm = PROJ_BLOCK
    nb = L // tm
    hb = tm // HALO
    const = lambda shape: pl.BlockSpec(shape, lambda b, i: (0,) * len(shape), pipeline_mode=pl.Buffered(1))
    tok = lambda w: pl.BlockSpec((None, tm, w), lambda b, i: (b, i, 0))
    out_w = (SSD_WIDTH, CONV_CH, GLA_KEY, GLA_KEY, GLA_VAL, GLA_VAL, LANES)
    return pl.pallas_call(
        _proj_kernel,
        grid=(B, nb),
        in_specs=[
            tok(D_MODEL),
            pl.BlockSpec((None, HALO, D_MODEL), lambda b, i: (b, jnp.maximum(i * hb - 1, 0), 0)),
            pl.BlockSpec((None, HALO, D_MODEL), lambda b, i: (b, jnp.minimum((i + 1) * hb, L // HALO - 1), 0)),
            const(wz.shape), const(wxbc.shape), const(wq.shape), const(wk.shape), const(wv.shape),
            const(wg.shape), const(wsm.shape), const(convw.shape), const(convb.shape), const(dtbias.shape),
        ],
        out_specs=[tok(w) for w in out_w],
        out_shape=[jax.ShapeDtypeStruct((B, L, w), F32) for w in out_w],
        scratch_shapes=[pltpu.VMEM((tm + 2 * HALO, CONV_CH), F32)],
        compiler_params=pltpu.CompilerParams(dimension_semantics=("parallel", "parallel"),
                                             vmem_limit_bytes=VMEM_LIMIT_BYTES),
        name="proj",
    )(x, x, x, wz, wxbc, wq, wk, wv, wg, wsm, convw, convb, dtbias)


def _ssd_chunk(reverse, r0, sm_ref, xbc_ref, alog_ref, tri_ref, e2_ref, state_ref, y_ref):
    q = SSD_CHUNK
    rows = slice(r0, r0 + q)
    lane_off = SM_DTB if reverse else SM_DTF
    sm = sm_ref[rows, :]
    lane1 = lax.broadcasted_iota(jnp.int32, (1, LANES), 1)
    a_neg = jnp.where(lane1 < SM_USED_DT, -jnp.exp(alog_ref[...]), 0.0)
    acs = _cumsum_rows(tri_ref[...], sm * a_neg)
    last = acs[0:1, :] if reverse else acs[q - 1:q, :]
    dte = jnp.exp(last - acs)
    eacs = jnp.exp(acs)
    e2 = e2_ref[...]
    dt_x = _expand_heads(sm, e2)
    dtd_x = _expand_heads(sm * dte, e2)
    eacs_x = _expand_heads(eacs, e2)
    cdec_x = _expand_heads(jnp.broadcast_to(jnp.exp(last), (SUBLANES, LANES)), e2)[0:1, :]
    xs = xbc_ref[rows, 0:SSD_WIDTH]
    xdt = (xs * dt_x).astype(BF16)
    xdd = (xs * dtd_x).astype(BF16)
    acs_t = acs.T
    li = lax.broadcasted_iota(jnp.int32, (q, q), 0)
    si = lax.broadcasted_iota(jnp.int32, (q, q), 1)
    mask = (si >= li) if reverse else (li >= si)
    lane_q = lax.broadcasted_iota(jnp.int32, (q, LANES), 1)
    for g in range(SSD_GROUPS):
        b0 = SSD_WIDTH + g * SSD_STATE
        c0 = SSD_WIDTH + SSD_GROUPS * SSD_STATE + g * SSD_STATE
        bg = xbc_ref[rows, b0:b0 + SSD_STATE].astype(BF16)
        cg = xbc_ref[rows, c0:c0 + SSD_STATE].astype(BF16)
        cb = _dot_nt(cg, bg)
        gl = slice(g * SSD_GW, (g + 1) * SSD_GW)
        s_in = state_ref[g]
        y_off = _dot(cg, s_in.astype(BF16)) * eacs_x[:, gl]
        for pr in range(SSD_GW // LANES):
            pl0 = g * SSD_GW + pr * LANES
            res = []
            for hh in range(LANES // SSD_HEADDIM):
                col = lane_off + g * SSD_HPG + pr * (LANES // SSD_HEADDIM) + hh
                seg = acs[:, col:col + 1] - acs_t[col:col + 1, :]
                lm = jnp.exp(jnp.where(mask, seg, -jnp.inf))
                m = (cb * lm).astype(BF16)
                res.append(_dot(m, xdt[:, pl0:pl0 + LANES]))
            y_pair = jnp.where(lane_q < SSD_HEADDIM, res[0], res[1])
            y_ref[rows, pl0:pl0 + LANES] = y_pair + y_off[:, pr * LANES:(pr + 1) * LANES]
        s_new = _dot_tn(bg, xdd[:, gl])
        state_ref[g] = s_in * cdec_x[:, gl] + s_new


def _gla_chunk(reverse, r0, sm_ref, q_ref, k_ref, v_ref, wgk_ref, bgk_ref, tri_ref, state_ref, o_ref):
    q = GLA_CHUNK
    rows = slice(r0, r0 + q)
    u = _dot(sm_ref[rows, :].astype(BF16), wgk_ref[...]) + bgk_ref[...]
    lg = (jnp.minimum(u, 0.0) - jnp.log1p(jnp.exp(-jnp.abs(u)))) / GLA_GATE_NORM
    bcs = _cumsum_rows(tri_ref[...], lg)
    last = bcs[0:1, :] if reverse else bcs[q - 1:q, :]
    kk = k_ref[rows, :]
    qt = (q_ref[rows, :] * jnp.exp(bcs)).astype(BF16)
    kt = (kk * jnp.exp(-bcs)).astype(BF16)
    ke = (kk * jnp.exp(last - bcs)).astype(BF16)
    dec = jnp.exp(last)
    li = lax.broadcasted_iota(jnp.int32, (q, q), 0)
    si = lax.broadcasted_iota(jnp.int32, (q, q), 1)
    mask = (si >= li) if reverse else (li >= si)
    for h in range(GLA_HEADS):
        kl = slice(h * GLA_DK, (h + 1) * GLA_DK)
        vl = slice(h * GLA_DV, (h + 1) * GLA_DV)
        vh = v_ref[rows, vl].astype(BF16)
        att = jnp.where(mask, _dot_nt(qt[:, kl], kt[:, kl]), 0.0).astype(BF16)
        st = state_ref[h]
        o_ref[rows, vl] = _dot(att, vh) + _dot_nt(qt[:, kl], st.astype(BF16))
        state_ref[h] = st * dec[:, kl] + _dot_tn(vh, ke[:, kl])


def _run_scans(reverse, sm_ref, xbc_ref, q_ref, k_ref, v_ref, alog_ref, wgk_ref, bgk_ref,
               tri_s_ref, tri_g_ref, e2_ref, ssd_state, gla_state, y_ref, o_ref):
    tb = sm_ref.shape[0]

    @pl.when(pl.program_id(1) == 0)
    def _():
        ssd_state[...] = jnp.zeros_like(ssd_state)
        gla_state[...] = jnp.zeros_like(gla_state)

    n_s = tb // SSD_CHUNK
    for c in (range(n_s - 1, -1, -1) if reverse else range(n_s)):
        _ssd_chunk(reverse, c * SSD_CHUNK, sm_ref, xbc_ref, alog_ref, tri_s_ref, e2_ref, ssd_state, y_ref)
    n_g = tb // GLA_CHUNK
    for c in (range(n_g - 1, -1, -1) if reverse else range(n_g)):
        _gla_chunk(reverse, c * GLA_CHUNK, sm_ref, q_ref, k_ref, v_ref, wgk_ref, bgk_ref, tri_g_ref,
                   gla_state, o_ref)


def _scan_bwd_kernel(sm_ref, xbc_ref, q_ref, k_ref, v_ref, alog_ref, wgk_ref, bgk_ref,
                     tri_s_ref, tri_g_ref, e2_ref, y_ref, o_ref, ssd_state, gla_state):
    _run_scans(True, sm_ref, xbc_ref, q_ref, k_ref, v_ref, alog_ref, wgk_ref, bgk_ref,
               tri_s_ref, tri_g_ref, e2_ref, ssd_state, gla_state, y_ref, o_ref)


def _scan_fwd_kernel(sm_ref, xbc_ref, q_ref, k_ref, v_ref, alog_ref, wgk_ref, bgk_ref,
                     tri_s_ref, tri_g_ref, e2_ref,
                     x_ref, z_ref, g_ref, yb_ref, ob_ref, dskip_ref, ssdg_ref, glag_ref, wout_ref,
                     ln_g_ref, ln_b_ref,
                     x1_ref, ssd_state, gla_state, y_scr, o_scr):
    _run_scans(False, sm_ref, xbc_ref, q_ref, k_ref, v_ref, alog_ref, wgk_ref, bgk_ref,
               tri_s_ref, tri_g_ref, e2_ref, ssd_state, gla_state, y_scr, o_scr)
    acc = ALPHA * x_ref[...]
    gw = SSD_WIDTH // SSD_GROUPS
    for g in range(SSD_GROUPS):
        sl = slice(g * gw, (g + 1) * gw)
        y = y_scr[:, sl] + yb_ref[:, sl] + dskip_ref[:, sl] * xbc_ref[:, sl]
        y = y * _silu(z_ref[:, sl])
        ms = jnp.mean(y * y, axis=-1, keepdims=True)
        yn = y * lax.rsqrt(ms + RMS_EPS) * ssdg_ref[:, sl]
        acc = acc + _dot(yn.astype(BF16), wout_ref[sl, :])
    for h in range(GLA_HEADS):
        sl = slice(h * GLA_DV, (h + 1) * GLA_DV)
        o = o_scr[:, sl] + ob_ref[:, sl]
        ms = jnp.mean(o * o, axis=-1, keepdims=True)
        on = o * lax.rsqrt(ms + RMS_EPS) * glag_ref[...] * _silu(g_ref[:, sl])
        acc = acc + _dot(on.astype(BF16), wout_ref[SSD_WIDTH + h * GLA_DV:SSD_WIDTH + (h + 1) * GLA_DV, :])
    x1_ref[...] = _layer_norm(acc, ln_g_ref[...], ln_b_ref[...])


def _scan_common_specs(L, tb, reverse, consts):
    nb = L // tb
    if reverse:
        tok = lambda w: pl.BlockSpec((None, tb, w), lambda b, i: (b, nb - 1 - i, 0))
    else:
        tok = lambda w: pl.BlockSpec((None, tb, w), lambda b, i: (b, i, 0))
    const = lambda a: pl.BlockSpec(a.shape, lambda b, i: (0,) * a.ndim, pipeline_mode=pl.Buffered(1))
    return tok, [const(a) for a in consts]


def _scan_bwd_call(sm, xbc, q, k, v, alog, wgk, bgk, tri_s, tri_g, e2):
    B, L, _ = sm.shape
    tb = SCAN_BLOCK
    consts = (alog, wgk, bgk, tri_s, tri_g, e2)
    tok, const_specs = _scan_common_specs(L, tb, True, consts)
    return pl.pallas_call(
        _scan_bwd_kernel,
        grid=(B, L // tb),
        in_specs=[tok(LANES), tok(CONV_CH), tok(GLA_KEY), tok(GLA_KEY), tok(GLA_VAL)] + const_specs,
        out_specs=[tok(SSD_WIDTH), tok(GLA_VAL)],
        out_shape=[jax.ShapeDtypeStruct((B, L, SSD_WIDTH), F32), jax.ShapeDtypeStruct((B, L, GLA_VAL), F32)],
        scratch_shapes=[pltpu.VMEM((SSD_GROUPS, SSD_STATE, SSD_GW), F32),
                        pltpu.VMEM((GLA_HEADS, GLA_DV, GLA_DK), F32)],
        compiler_params=pltpu.CompilerParams(dimension_semantics=("arbitrary", "arbitrary"),
                                             vmem_limit_bytes=VMEM_LIMIT_BYTES),
        name="scan_bwd",
    )(sm, xbc, q, k, v, *consts)


def _scan_fwd_call(sm, xbc, q, k, v, alog, wgk, bgk, tri_s, tri_g, e2,
                   x, z, g, yb, ob, dskip, ssdg, glag, wout, ln_g, ln_b):
    B, L, _ = sm.shape
    tb = SCAN_BLOCK
    consts_a = (alog, wgk, bgk, tri_s, tri_g, e2)
    consts_b = (dskip, ssdg, glag, wout, ln_g, ln_b)
    tok, const_a = _scan_common_specs(L, tb, False, consts_a)
    _, const_b = _scan_common_specs(L, tb, False, consts_b)
    return pl.pallas_call(
        _scan_fwd_kernel,
        grid=(B, L // tb),
        in_specs=([tok(LANES), tok(CONV_CH), tok(GLA_KEY), tok(GLA_KEY), tok(GLA_VAL)] + const_a
                  + [tok(D_MODEL), tok(SSD_WIDTH), tok(GLA_VAL), tok(SSD_WIDTH), tok(GLA_VAL)] + const_b),
        out_specs=tok(D_MODEL),
        out_shape=jax.ShapeDtypeStruct((B, L, D_MODEL), F32),
        scratch_shapes=[pltpu.VMEM((SSD_GROUPS, SSD_STATE, SSD_GW), F32),
                        pltpu.VMEM((GLA_HEADS, GLA_DV, GLA_DK), F32),
                        pltpu.VMEM((tb, SSD_WIDTH), F32),
                        pltpu.VMEM((tb, GLA_VAL), F32)],
        compiler_params=pltpu.CompilerParams(dimension_semantics=("arbitrary", "arbitrary"),
                                             vmem_limit_bytes=VMEM_LIMIT_BYTES),
        name="scan_fwd",
    )(sm, xbc, q, k, v, *consts_a, x, z, g, yb, ob, *consts_b)


def _memkv_kernel(mem_ref, wk_ref, wv_ref, k_ref, v_ref):
    m = mem_ref[...].astype(BF16)
    k_ref[...] = _dot(m, wk_ref[...]).astype(BF16)
    v_ref[...] = _dot(m, wv_ref[...]).astype(BF16)


def _memkv_call(mem, wk, wv):
    B, M, _ = mem.shape
    blk = pl.BlockSpec((None, M, D_MODEL), lambda b: (b, 0, 0))
    const = lambda a: pl.BlockSpec(a.shape, lambda b: (0,) * a.ndim, pipeline_mode=pl.Buffered(1))
    return pl.pallas_call(
        _memkv_kernel,
        grid=(B,),
        in_specs=[blk, const(wk), const(wv)],
        out_specs=[blk, blk],
        out_shape=[jax.ShapeDtypeStruct((B, M, D_MODEL), BF16)] * 2,
        compiler_params=pltpu.CompilerParams(dimension_semantics=("parallel",),
                                             vmem_limit_bytes=VMEM_LIMIT_BYTES),
        name="memkv",
    )(mem, wk, wv)


def _attn_mlp_kernel(x_ref, mk_ref, mv_ref, wq_ref, wo_ref, ln2g_ref, ln2b_ref, w1_ref, w2_ref,
                     ln3g_ref, ln3b_ref, out_ref):
    x1 = x_ref[...]
    qf = _dot(x1.astype(BF16), wq_ref[...])
    acc = ALPHA * x1
    for h in range(MEM_HEADS):
        sl = slice(h * MEM_HD, (h + 1) * MEM_HD)
        s = _dot_nt(qf[:, sl].astype(BF16), mk_ref[:, sl]) * (MEM_HD ** -0.5)
        s = s - jnp.max(s, axis=-1, keepdims=True)
        e = jnp.exp(s)
        p = e / jnp.sum(e, axis=-1, keepdims=True)
        oh = _dot(p.astype(BF16), mv_ref[:, sl])
        acc = acc + _dot(oh.astype(BF16), wo_ref[sl, :])
    x2 = _layer_norm(acc, ln2g_ref[...], ln2b_ref[...])
    x2b = x2.astype(BF16)
    acc = ALPHA * x2
    for c in range(0, D_FF, FF_CHUNK):
        hdn = jnp.maximum(_dot(x2b, w1_ref[:, c:c + FF_CHUNK]), 0.0)
        acc = acc + _dot((hdn * hdn).astype(BF16), w2_ref[c:c + FF_CHUNK, :])
    out_ref[...] = _layer_norm(acc, ln3g_ref[...], ln3b_ref[...])


def _attn_mlp_call(x1, mk, mv, wq, wo, ln2g, ln2b, w1, w2, ln3g, ln3b):
    B, L, _ = x1.shape
    M = mk.shape[1]
    tm = ATTN_BLOCK
    tok = pl.BlockSpec((None, tm, D_MODEL), lambda b, i: (b, i, 0))
    memb = pl.BlockSpec((None, M, D_MODEL), lambda b, i: (b, 0, 0))
    const = lambda a: pl.BlockSpec(a.shape, lambda b, i: (0,) * a.ndim, pipeline_mode=pl.Buffered(1))
    return pl.pallas_call(
        _attn_mlp_kernel,
        grid=(B, L // tm),
        in_specs=[tok, memb, memb, const(wq), const(wo), const(ln2g), const(ln2b), const(w1), const(w2),
                  const(ln3g), const(ln3b)],
        out_specs=tok,
        out_shape=jax.ShapeDtypeStruct((B, L, D_MODEL), F32),
        compiler_params=pltpu.CompilerParams(dimension_semantics=("parallel", "parallel"),
                                             vmem_limit_bytes=VMEM_LIMIT_BYTES),
        name="attn_mlp",
    )(x1, mk, mv, wq, wo, ln2g, ln2b, w1, w2, ln3g, ln3b)


def _prepare(w_in, conv_w, conv_b, a_log_f, a_log_b, dt_bias_f, dt_bias_b, d_skip, ssd_norm_g,
             w_gk_f, b_gk_f, w_gk_b, b_gk_b, gla_norm_g, w_out, ln1_g, ln1_b,
             w_mq, w_mk, w_mv, w_mo, ln2_g, ln2_b, w_ff1, w_ff2, ln3_g, ln3_b):
    offs = [0]
    for s in IN_SIZES:
        offs.append(offs[-1] + s)
    col = lambda i: w_in[:, offs[i]:offs[i + 1]]
    wz = col(0).astype(BF16)
    wxbc = jnp.concatenate([col(1), col(2), col(3)], axis=1).astype(BF16)
    wq, wk, wv, wg = (col(i).astype(BF16) for i in (6, 7, 8, 9))
    pad = LANES - 2 * SSD_HEADS - GLA_LOWRANK
    wsm = jnp.concatenate([col(4), col(5), col(10), jnp.zeros((D_MODEL, pad), F32)], axis=1).astype(BF16)
    row = lambda a: a.reshape(1, -1).astype(F32)
    zpad = jnp.zeros((1, LANES - 2 * SSD_HEADS), F32)
    dtbias = jnp.concatenate([row(dt_bias_f), row(dt_bias_b), zpad], axis=1)
    alog = jnp.concatenate([row(a_log_f), row(a_log_b), zpad], axis=1)

    def gk_pad(w):
        return jnp.zeros((LANES, GLA_KEY), F32).at[SM_LR:SM_LR + GLA_LOWRANK].set(w).astype(BF16)

    def tri3(qn, reverse):
        r = jnp.arange(qn)[:, None]
        c = jnp.arange(qn)[None, :]
        t = (c >= r) if reverse else (r >= c)
        return jnp.tile(t.astype(BF16), (1, 3))

    def e2(lane_off):
        r = jnp.arange(LANES)[:, None]
        c = jnp.arange(SSD_WIDTH)[None, :]
        e = (c // SSD_HEADDIM == r - lane_off).astype(BF16)
        return jnp.concatenate([e, e], axis=0)

    return dict(
        proj=(wz, wxbc, wq, wk, wv, wg, wsm, conv_w.astype(F32), row(conv_b), dtbias),
        bwd=(alog, gk_pad(w_gk_b), row(b_gk_b), tri3(SSD_CHUNK, True), tri3(GLA_CHUNK, True), e2(SM_DTB)),
        fwd=(alog, gk_pad(w_gk_f), row(b_gk_f), tri3(SSD_CHUNK, False), tri3(GLA_CHUNK, False), e2(SM_DTF)),
        comb=(row(jnp.repeat(d_skip, SSD_HEADDIM)), row(ssd_norm_g), row(gla_norm_g), w_out.astype(BF16),
              row(ln1_g), row(ln1_b)),
        memkv=(w_mk.astype(BF16), w_mv.astype(BF16)),
        attn=(w_mq.astype(BF16), w_mo.astype(BF16), row(ln2_g), row(ln2_b), w_ff1.astype(BF16),
              w_ff2.astype(BF16), row(ln3_g), row(ln3_b)),
    )


def _encoder_layer(x, mem, p):
    z, xbc, q, k, v, g, sm = _proj_call(x, *p["proj"])
    yb, ob = _scan_bwd_call(sm, xbc, q, k, v, *p["bwd"])
    x1 = _scan_fwd_call(sm, xbc, q, k, v, *p["fwd"], x, z, g, yb, ob, *p["comb"])
    mk, mv = _memkv_call(mem, *p["memkv"])
    return _attn_mlp_call(x1, mk, mv, *p["attn"])


def kernel(x_prompt, x_sample, mem_prompt, mem_sample, w_in, conv_w, conv_b, a_log_f, a_log_b, dt_bias_f,
           dt_bias_b, d_skip, ssd_norm_g, w_gk_f, b_gk_f, w_gk_b, b_gk_b, gla_norm_g, w_out, ln1_g, ln1_b,
           w_mq, w_mk, w_mv, w_mo, ln2_g, ln2_b, w_ff1, w_ff2, ln3_g, ln3_b):
    params = (w_in, conv_w, conv_b, a_log_f, a_log_b, dt_bias_f, dt_bias_b, d_skip, ssd_norm_g,
              w_gk_f, b_gk_f, w_gk_b, b_gk_b, gla_norm_g, w_out, ln1_g, ln1_b,
              w_mq, w_mk, w_mv, w_mo, ln2_g, ln2_b, w_ff1, w_ff2, ln3_g, ln3_b)
    assert all(a.shape[0] == DEPTH for a in params)
    p = _prepare(*[a[0] for a in params])
    return (_encoder_layer(x_prompt, mem_prompt, p), _encoder_layer(x_sample, mem_sample, p))
```

```python
import jax
import jax.numpy as jnp
from jax import lax
from jax.experimental import pallas as pl
from jax.experimental.pallas import tpu as pltpu

F32 = jnp.float32
BF16 = jnp.bfloat16

D_MODEL = 1024
SSD_WIDTH = 1024
SSD_HEADDIM = 64
SSD_HEADS = 16
SSD_GROUPS = 2
SSD_HPG = 8
SSD_STATE = 128
SSD_CHUNK = 128
SSD_GW = SSD_HPG * SSD_HEADDIM
CONV_K = 5
CONV_CH = SSD_WIDTH + 2 * SSD_GROUPS * SSD_STATE
BC_WIDTH = 2 * SSD_GROUPS * SSD_STATE
GLA_HEADS = 4
GLA_KEY = 512
GLA_VAL = 1024
GLA_DK = 128
GLA_DV = 256
GLA_LOWRANK = 16
GLA_GATE_NORM = 16.0
GLA_CHUNK = 64
MIX_WIDTH = SSD_WIDTH + GLA_VAL
IN_SIZES = (SSD_WIDTH, SSD_WIDTH, SSD_GROUPS * SSD_STATE, SSD_GROUPS * SSD_STATE, SSD_HEADS, SSD_HEADS,
            GLA_KEY, GLA_KEY, GLA_VAL, GLA_VAL, GLA_LOWRANK)
MEM_HEADS = 4
MEM_HD = 256
D_FF = 4096
DEPTH = 1
ALPHA = (2 * DEPTH) ** 0.25
LN_EPS = 1e-5
RMS_EPS = 1e-5

LANES = 128
SUBLANES = 8
VMEM_LIMIT_BYTES = 56 * 1024 * 1024

SM_DTF = 0
SM_DTB = SSD_HEADS
SM_LR = 2 * SSD_HEADS
SM_USED_DT = 2 * SSD_HEADS

HALO = SUBLANES
CONV_L = (CONV_K - 1) // 2

PROJ_BLOCK = 512
CONV_COLS = 512
SCAN_BLOCK = 256
ATTN_BLOCK = 512
ATTN_SUB = 256
FF_CHUNK = 1024


def _dot(a, b):
    return jnp.dot(a, b, preferred_element_type=F32)


def _dot_nt(a, b):
    return lax.dot_general(a, b, (((1,), (1,)), ((), ())), preferred_element_type=F32)


def _dot_tn(a, b):
    return lax.dot_general(a, b, (((0,), (0,)), ((), ())), preferred_element_type=F32)


def _sigmoid(x):
    return 1.0 / (1.0 + jnp.exp(-x))


def _silu(x):
    return x * _sigmoid(x)


def _softplus(x):
    return jnp.maximum(x, 0.0) + jnp.log1p(jnp.exp(-jnp.abs(x)))


def _split3(a):
    hi = a.astype(BF16)
    r = a - hi.astype(F32)
    mid = r.astype(BF16)
    lo = (r - mid.astype(F32)).astype(BF16)
    return hi, mid, lo


def _cumsum_rows(tri3, a):
    hi, mid, lo = _split3(a)
    return _dot(tri3, jnp.concatenate([hi, mid, lo], axis=0))


def _expand_heads(a, e2):
    hi = a.astype(BF16)
    lo = (a - hi.astype(F32)).astype(BF16)
    return _dot(jnp.concatenate([hi, lo], axis=1), e2)


def _layer_norm(h, g, b):
    mu = jnp.mean(h, axis=-1, keepdims=True)
    d = h - mu
    var = jnp.mean(d * d, axis=-1, keepdims=True)
    return d * lax.rsqrt(var + LN_EPS) * g + b


def _proj_kernel(x_ref, xp_ref, xn_ref, wz_ref, wxbc_ref, wq_ref, wk_ref, wv_ref, wg_ref, wsm_ref,
                 convw_ref, convb_ref, dtbias_ref,
                 z_ref, xs_ref, bc_ref, q_ref, k_ref, v_ref, g_ref, sm_ref, ext_ref):
    i = pl.program_id(1)
    nb = pl.num_programs(1)
    tm = x_ref.shape[0]
    xb = x_ref[...].astype(BF16)
    xpb = xp_ref[...].astype(BF16)
    xnb = xn_ref[...].astype(BF16)
    has_prev = i > 0
    has_next = i < nb - 1
    cw = CONV_COLS
    for c in range(0, CONV_CH, cw):
        w = wxbc_ref[:, c:c + cw]
        up = jnp.where(has_prev, _dot(xpb, w), 0.0)
        uc = _dot(xb, w)
        un = jnp.where(has_next, _dot(xnb, w), 0.0)
        for t in range(cw // LANES):
            ls = slice(t * LANES, (t + 1) * LANES)
            ext_ref[c // LANES + t, 0:HALO, :] = up[:, ls]
            ext_ref[c // LANES + t, HALO:HALO + tm, :] = uc[:, ls]
            ext_ref[c // LANES + t, HALO + tm:HALO + tm + HALO, :] = un[:, ls]

    def conv_tile(ct):
        ls = slice(ct * LANES, (ct + 1) * LANES)
        acc = jnp.broadcast_to(convb_ref[:, ls], (tm, LANES))
        for j in range(CONV_K):
            r0 = HALO - CONV_L + j
            acc = acc + convw_ref[j:j + 1, ls] * ext_ref[ct, r0:r0 + tm, :]
        return _silu(acc)

    n_xs = SSD_WIDTH // LANES
    for ct in range(n_xs):
        xs_ref[:, ct * LANES:(ct + 1) * LANES] = conv_tile(ct)
    for ct in range(n_xs, CONV_CH // LANES):
        bc_ref[:, (ct - n_xs) * LANES:(ct - n_xs + 1) * LANES] = conv_tile(ct).astype(BF16)

    z_ref[...] = _dot(xb, wz_ref[...])
    q_ref[...] = _dot(xb, wq_ref[...]) * (GLA_DK ** -0.5)
    k_ref[...] = _dot(xb, wk_ref[...])
    v_ref[...] = _dot(xb, wv_ref[...]).astype(BF16)
    g_ref[...] = _dot(xb, wg_ref[...])
    sm = _dot(xb, wsm_ref[...])
    lane = lax.broadcasted_iota(jnp.int32, sm.shape, 1)
    sm_ref[...] = jnp.where(lane < SM_USED_DT, _softplus(sm + dtbias_ref[...]), sm)


def _proj_call(x, wz, wxbc, wq, wk, wv, wg, wsm, convw, convb, dtbias):
    B, L, _ = x.shape
    tm = PROJ_BLOCK
    nb = L // tm
    hb = tm // HALO
    const = lambda shape: pl.BlockSpec(shape, lambda b, i: (0,) * len(shape), pipeline_mode=pl.Buffered(1))
    tok = lambda w: pl.BlockSpec((None, tm, w), lambda b, i: (b, i, 0))
    outs = ((SSD_WIDTH, F32), (SSD_WIDTH, F32), (BC_WIDTH, BF16), (GLA_KEY, F32), (GLA_KEY, F32), (GLA_VAL, BF16),
            (GLA_VAL, F32), (LANES, F32))
    return pl.pallas_call(
        _proj_kernel,
        grid=(B, nb),
        in_specs=[
            tok(D_MODEL),
            pl.BlockSpec((None, HALO, D_MODEL), lambda b, i: (b, jnp.maximum(i * hb - 1, 0), 0)),
            pl.BlockSpec((None, HALO, D_MODEL), lambda b, i: (b, jnp.minimum((i + 1) * hb, L // HALO - 1), 0)),
            const(wz.shape), const(wxbc.shape), const(wq.shape), const(wk.shape), const(wv.shape),
            const(wg.shape), const(wsm.shape), const(convw.shape), const(convb.shape), const(dtbias.shape),
        ],
        out_specs=[tok(w) for w, _ in outs],
        out_shape=[jax.ShapeDtypeStruct((B, L, w), dt) for w, dt in outs],
        scratch_shapes=[pltpu.VMEM((CONV_CH // LANES, tm + 2 * HALO, LANES), F32)],
        compiler_params=pltpu.CompilerParams(dimension_semantics=("parallel", "parallel"),
                                             vmem_limit_bytes=VMEM_LIMIT_BYTES),
        name="proj",
    )(x, x, x, wz, wxbc, wq, wk, wv, wg, wsm, convw, convb, dtbias)


def _ssd_prep(reverse, r0, sm_ref, xs_ref, alog_ref, tri_ref, e2_ref):
    q = SSD_CHUNK
    rows = slice(r0, r0 + q)
    sm = sm_ref[rows, :]
    lane1 = lax.broadcasted_iota(jnp.int32, (1, LANES), 1)
    a_neg = jnp.where(lane1 < SM_USED_DT, -jnp.exp(alog_ref[...]), 0.0)
    acs = _cumsum_rows(tri_ref[...], sm * a_neg)
    last = acs[0:1, :] if reverse else acs[q - 1:q, :]
    dte = jnp.exp(last - acs)
    e2 = e2_ref[...]
    dt_x = _expand_heads(sm, e2)
    dtd_x = _expand_heads(sm * dte, e2)
    eacs_x = _expand_heads(jnp.exp(acs), e2)
    cdec_x = _expand_heads(jnp.broadcast_to(jnp.exp(last), (SUBLANES, LANES)), e2)[0:1, :]
    xs = xs_ref[rows, :]
    xdt = xs * dt_x
    lane_w = lax.broadcasted_iota(jnp.int32, (q, SSD_WIDTH), 1)
    even = (lane_w & SSD_HEADDIM) == 0
    xdt_e = jnp.where(even, xdt, 0.0).astype(BF16)
    xdt_o = jnp.where(even, 0.0, xdt).astype(BF16)
    xdd = (xs * dtd_x).astype(BF16)
    return dict(acs=acs, acs_t=acs.T, xdt_e=xdt_e, xdt_o=xdt_o, xdd=xdd, eacs_x=eacs_x, cdec_x=cdec_x)


def _ssd_intra(reverse, r0, p, bc_ref, y_ref):
    q = SSD_CHUNK
    rows = slice(r0, r0 + q)
    lane_off = SM_DTB if reverse else SM_DTF
    li = lax.broadcasted_iota(jnp.int32, (q, q), 0)
    si = lax.broadcasted_iota(jnp.int32, (q, q), 1)
    mask = (si >= li) if reverse else (li >= si)
    acs, acs_t = p["acs"], p["acs_t"]
    cgs, snews = [], []
    for g in range(SSD_GROUPS):
        b0 = g * SSD_STATE
        c0 = SSD_GROUPS * SSD_STATE + g * SSD_STATE
        bg = bc_ref[rows, b0:b0 + SSD_STATE]
        cg = bc_ref[rows, c0:c0 + SSD_STATE]
        cb = _dot_nt(cg, bg)
        for pr in range(SSD_GW // LANES):
            pl0 = g * SSD_GW + pr * LANES
            ms = []
            for hh in range(LANES // SSD_HEADDIM):
                col = lane_off + g * SSD_HPG + pr * (LANES // SSD_HEADDIM) + hh
                seg = acs[:, col:col + 1] - acs_t[col:col + 1, :]
                lm = jnp.exp(jnp.where(mask, seg, -jnp.inf))
                ms.append((cb * lm).astype(BF16))
            rhs = jnp.concatenate([p["xdt_e"][:, pl0:pl0 + LANES], p["xdt_o"][:, pl0:pl0 + LANES]], axis=0)
            y_ref[rows, pl0:pl0 + LANES] = _dot(jnp.concatenate(ms, axis=1), rhs)
        gl = slice(g * SSD_GW, (g + 1) * SSD_GW)
        cgs.append(cg)
        snews.append(_dot_tn(bg, p["xdd"][:, gl]))
    return cgs, snews


def _ssd_carry(r0, p, cgs, snews, state_ref, y_ref):
    rows = slice(r0, r0 + SSD_CHUNK)
    for g in range(SSD_GROUPS):
        gl = slice(g * SSD_GW, (g + 1) * SSD_GW)
        s_in = state_ref[g]
        y_ref[rows, gl] = y_ref[rows, gl] + _dot(cgs[g], s_in.astype(BF16)) * p["eacs_x"][:, gl]
        state_ref[g] = s_in * p["cdec_x"][:, gl] + snews[g]


def _gla_prep(reverse, sm_ref, q_ref, k_ref, wgk_ref, bgk_ref, tri_ref):
    tb = sm_ref.shape[0]
    q = GLA_CHUNK
    u = _dot(sm_ref[...].astype(BF16), wgk_ref[...]) + bgk_ref[...]
    lg = (jnp.minimum(u, 0.0) - jnp.log1p(jnp.exp(-jnp.abs(u)))) / GLA_GATE_NORM
    hi, mid, lo = _split3(lg)
    tri = tri_ref[...]
    bcs_c, last_c = [], []
    for r0 in range(0, tb, q):
        rows = slice(r0, r0 + q)
        b = _dot(tri, jnp.concatenate([hi[rows], mid[rows], lo[rows]], axis=0))
        bcs_c.append(b)
        last_c.append(b[0:1, :] if reverse else b[q - 1:q, :])
    bcs = jnp.concatenate(bcs_c, axis=0)
    last_b = jnp.concatenate([jnp.broadcast_to(l, (q, GLA_KEY)) for l in last_c], axis=0)
    kk = k_ref[...]
    qt = (q_ref[...] * jnp.exp(bcs)).astype(BF16)
    kt = (kk * jnp.exp(-bcs)).astype(BF16)
    ke = (kk * jnp.exp(last_b - bcs)).astype(BF16)
    dec_cols = [[jnp.broadcast_to(jnp.exp(l[:, h * GLA_DK:(h + 1) * GLA_DK]), (GLA_DK, GLA_DK)).T
                 for h in range(GLA_HEADS)] for l in last_c]
    return qt, kt, ke, dec_cols


def _gla_intra(reverse, r0, qt, kt, ke, v_ref):
    q = GLA_CHUNK
    rows = slice(r0, r0 + q)
    li = lax.broadcasted_iota(jnp.int32, (q, q), 0)
    si = lax.broadcasted_iota(jnp.int32, (q, q), 1)
    mask = (si >= li) if reverse else (li >= si)
    atts, ups = [], []
    for h in range(GLA_HEADS):
        kl = slice(h * GLA_DK, (h + 1) * GLA_DK)
        vl = slice(h * GLA_DV, (h + 1) * GLA_DV)
        atts.append(jnp.where(mask, _dot_nt(qt[rows, kl], kt[rows, kl]), 0.0).astype(BF16))
        ups.append(_dot_tn(ke[rows, kl], v_ref[rows, vl]))
    return atts, ups


def _gla_carry(r0, qt, dec_cols, atts, ups, v_ref, state_ref, o_ref):
    rows = slice(r0, r0 + GLA_CHUNK)
    for h in range(GLA_HEADS):
        kl = slice(h * GLA_DK, (h + 1) * GLA_DK)
        vl = slice(h * GLA_DV, (h + 1) * GLA_DV)
        st = state_ref[h]
        lhs = jnp.concatenate([qt[rows, kl], atts[h]], axis=1)
        rhs = jnp.concatenate([st.astype(BF16), v_ref[rows, vl]], axis=0)
        o_ref[rows, vl] = _dot(lhs, rhs)
        dcol = dec_cols[h]
        state_ref[h] = st * jnp.concatenate([dcol, dcol], axis=1) + ups[h]


def _run_scans(reverse, sm_ref, xs_ref, bc_ref, q_ref, k_ref, v_ref, alog_ref, wgk_ref, bgk_ref,
               tri_s_ref, tri_g_ref, e2_ref, ssd_state, gla_state, y_ref, o_ref):
    tb = sm_ref.shape[0]

    @pl.when(pl.program_id(1) == 0)
    def _():
        ssd_state[...] = jnp.zeros_like(ssd_state)
        gla_state[...] = jnp.zeros_like(gla_state)

    s_chunks = list(range(0, tb, SSD_CHUNK))
    g_chunks = list(range(0, tb, GLA_CHUNK))
    preps = [_ssd_prep(reverse, r0, sm_ref, xs_ref, alog_ref, tri_s_ref, e2_ref) for r0 in s_chunks]
    qt, kt, ke, dec_cols = _gla_prep(reverse, sm_ref, q_ref, k_ref, wgk_ref, bgk_ref, tri_g_ref)
    s_intra = [_ssd_intra(reverse, r0, p, bc_ref, y_ref) for r0, p in zip(s_chunks, preps)]
    g_intra = [_gla_intra(reverse, r0, qt, kt, ke, v_ref) for r0 in g_chunks]
    s_order = range(len(s_chunks) - 1, -1, -1) if reverse else range(len(s_chunks))
    g_order = range(len(g_chunks) - 1, -1, -1) if reverse else range(len(g_chunks))
    for c in s_order:
        _ssd_carry(s_chunks[c], preps[c], *s_intra[c], ssd_state, y_ref)
    for c in g_order:
        _gla_carry(g_chunks[c], qt, dec_cols[c], *g_intra[c], v_ref, gla_state, o_ref)


def _scan_bwd_kernel(sm_ref, xs_ref, bc_ref, q_ref, k_ref, v_ref, alog_ref, wgk_ref, bgk_ref,
                     tri_s_ref, tri_g_ref, e2_ref, y_ref, o_ref, ssd_state, gla_state):
    _run_scans(True, sm_ref, xs_ref, bc_ref, q_ref, k_ref, v_ref, alog_ref, wgk_ref, bgk_ref,
               tri_s_ref, tri_g_ref, e2_ref, ssd_state, gla_state, y_ref, o_ref)


def _scan_fwd_kernel(sm_ref, xs_ref, bc_ref, q_ref, k_ref, v_ref, alog_ref, wgk_ref, bgk_ref,
                     tri_s_ref, tri_g_ref, e2_ref,
                     x_ref, z_ref, g_ref, yb_ref, ob_ref, dskip_ref, ssdg_ref, glag_ref, wout_ref,
                     ln_g_ref, ln_b_ref,
                     x1_ref, ssd_state, gla_state, y_scr, o_scr):
    _run_scans(False, sm_ref, xs_ref, bc_ref, q_ref, k_ref, v_ref, alog_ref, wgk_ref, bgk_ref,
               tri_s_ref, tri_g_ref, e2_ref, ssd_state, gla_state, y_scr, o_scr)
    acc = ALPHA * x_ref[...]
    gw = SSD_WIDTH // SSD_GROUPS
    for g in range(SSD_GROUPS):
        sl = slice(g * gw, (g + 1) * gw)
        y = y_scr[:, sl] + yb_ref[:, sl] + dskip_ref[:, sl] * xs_ref[:, sl]
        y = y * _silu(z_ref[:, sl])
        ms = jnp.mean(y * y, axis=-1, keepdims=True)
        yn = y * lax.rsqrt(ms + RMS_EPS) * ssdg_ref[:, sl]
        acc = acc + _dot(yn.astype(BF16), wout_ref[sl, :])
    for h in range(GLA_HEADS):
        sl = slice(h * GLA_DV, (h + 1) * GLA_DV)
        o = o_scr[:, sl] + ob_ref[:, sl]
        ms = jnp.mean(o * o, axis=-1, keepdims=True)
        on = o * lax.rsqrt(ms + RMS_EPS) * glag_ref[...] * _silu(g_ref[:, sl])
        acc = acc + _dot(on.astype(BF16), wout_ref[SSD_WIDTH + h * GLA_DV:SSD_WIDTH + (h + 1) * GLA_DV, :])
    x1_ref[...] = _layer_norm(acc, ln_g_ref[...], ln_b_ref[...])


def _scan_common_specs(L, tb, reverse, consts):
    nb = L // tb
    if reverse:
        tok = lambda w: pl.BlockSpec((None, tb, w), lambda b, i: (b, nb - 1 - i, 0))
    else:
        tok = lambda w: pl.BlockSpec((None, tb, w), lambda b, i: (b, i, 0))
    const = lambda a: pl.BlockSpec(a.shape, lambda b, i: (0,) * a.ndim, pipeline_mode=pl.Buffered(1))
    return tok, [const(a) for a in consts]


def _scan_bwd_call(sm, xs, bc, q, k, v, alog, wgk, bgk, tri_s, tri_g, e2):
    B, L, _ = sm.shape
    tb = SCAN_BLOCK
    consts = (alog, wgk, bgk, tri_s, tri_g, e2)
    tok, const_specs = _scan_common_specs(L, tb, True, consts)
    return pl.pallas_call(
        _scan_bwd_kernel,
        grid=(B, L // tb),
        in_specs=[tok(LANES), tok(SSD_WIDTH), tok(BC_WIDTH), tok(GLA_KEY), tok(GLA_KEY), tok(GLA_VAL)] + const_specs,
        out_specs=[tok(SSD_WIDTH), tok(GLA_VAL)],
        out_shape=[jax.ShapeDtypeStruct((B, L, SSD_WIDTH), F32), jax.ShapeDtypeStruct((B, L, GLA_VAL), F32)],
        scratch_shapes=[pltpu.VMEM((SSD_GROUPS, SSD_STATE, SSD_GW), F32),
                        pltpu.VMEM((GLA_HEADS, GLA_DK, GLA_DV), F32)],
        compiler_params=pltpu.CompilerParams(dimension_semantics=("arbitrary", "arbitrary"),
                                             vmem_limit_bytes=VMEM_LIMIT_BYTES),
        name="scan_bwd",
    )(sm, xs, bc, q, k, v, *consts)


def _scan_fwd_call(sm, xs, bc, q, k, v, alog, wgk, bgk, tri_s, tri_g, e2,
                   x, z, g, yb, ob, dskip, ssdg, glag, wout, ln_g, ln_b):
    B, L, _ = sm.shape
    tb = SCAN_BLOCK
    consts_a = (alog, wgk, bgk, tri_s, tri_g, e2)
    consts_b = (dskip, ssdg, glag, wout, ln_g, ln_b)
    tok, const_a = _scan_common_specs(L, tb, False, consts_a)
    _, const_b = _scan_common_specs(L, tb, False, consts_b)
    return pl.pallas_call(
        _scan_fwd_kernel,
        grid=(B, L // tb),
        in_specs=([tok(LANES), tok(SSD_WIDTH), tok(BC_WIDTH), tok(GLA_KEY), tok(GLA_KEY), tok(GLA_VAL)] + const_a
                  + [tok(D_MODEL), tok(SSD_WIDTH), tok(GLA_VAL), tok(SSD_WIDTH), tok(GLA_VAL)] + const_b),
        out_specs=tok(D_MODEL),
        out_shape=jax.ShapeDtypeStruct((B, L, D_MODEL), F32),
        scratch_shapes=[pltpu.VMEM((SSD_GROUPS, SSD_STATE, SSD_GW), F32),
                        pltpu.VMEM((GLA_HEADS, GLA_DK, GLA_DV), F32),
                        pltpu.VMEM((tb, SSD_WIDTH), F32),
                        pltpu.VMEM((tb, GLA_VAL), F32)],
        compiler_params=pltpu.CompilerParams(dimension_semantics=("arbitrary", "arbitrary"),
                                             vmem_limit_bytes=VMEM_LIMIT_BYTES),
        name="scan_fwd",
    )(sm, xs, bc, q, k, v, *consts_a, x, z, g, yb, ob, *consts_b)


def _memkv_kernel(mem_ref, wk_ref, wv_ref, k_ref, v_ref):
    m = mem_ref[...].astype(BF16)
    k_ref[...] = _dot(m, wk_ref[...]).astype(BF16)
    v_ref[...] = _dot(m, wv_ref[...]).astype(BF16)


def _memkv_call(mem, wk, wv):
    B, M, _ = mem.shape
    blk = pl.BlockSpec((None, M, D_MODEL), lambda b: (b, 0, 0))
    const = lambda a: pl.BlockSpec(a.shape, lambda b: (0,) * a.ndim, pipeline_mode=pl.Buffered(1))
    return pl.pallas_call(
        _memkv_kernel,
        grid=(B,),
        in_specs=[blk, const(wk), const(wv)],
        out_specs=[blk, blk],
        out_shape=[jax.ShapeDtypeStruct((B, M, D_MODEL), BF16)] * 2,
        compiler_params=pltpu.CompilerParams(dimension_semantics=("parallel",),
                                             vmem_limit_bytes=VMEM_LIMIT_BYTES),
        name="memkv",
    )(mem, wk, wv)


def _attn_part(subs, x_ref, mk_ref, mv_ref, wq_ref, wo_ref, ln2g_ref, ln2b_ref):
    x1s = [x_ref[rows, :] for rows in subs]
    qfs = [_dot(x1.astype(BF16), wq_ref[...]) for x1 in x1s]
    accs = [ALPHA * x1 for x1 in x1s]
    for h in range(MEM_HEADS):
        sl = slice(h * MEM_HD, (h + 1) * MEM_HD)
        ss = [_dot_nt(qf[:, sl].astype(BF16), mk_ref[:, sl]) * (MEM_HD ** -0.5) for qf in qfs]
        ps = []
        for s in ss:
            e = jnp.exp(s - jnp.max(s, axis=-1, keepdims=True))
            ps.append((e / jnp.sum(e, axis=-1, keepdims=True)).astype(BF16))
        ohs = [_dot(p, mv_ref[:, sl]).astype(BF16) for p in ps]
        accs = [acc + _dot(oh, wo_ref[sl, :]) for acc, oh in zip(accs, ohs)]
    return [_layer_norm(acc, ln2g_ref[...], ln2b_ref[...]) for acc in accs]


def _mlp_part(subs, x2s, w1_ref, w2_ref, ln3g_ref, ln3b_ref, out_ref):
    x2bs = [x2.astype(BF16) for x2 in x2s]
    accs = [ALPHA * x2 for x2 in x2s]
    for c in range(0, D_FF, FF_CHUNK):
        hs = [jnp.maximum(_dot(x2b, w1_ref[:, c:c + FF_CHUNK]), 0.0) for x2b in x2bs]
        accs = [acc + _dot((hdn * hdn).astype(BF16), w2_ref[c:c + FF_CHUNK, :]) for acc, hdn in zip(accs, hs)]
    for rows, acc in zip(subs, accs):
        out_ref[rows, :] = _layer_norm(acc, ln3g_ref[...], ln3b_ref[...])


def _attn_mlp_kernel(x_ref, mk_ref, mv_ref, wq_ref, wo_ref, ln2g_ref, ln2b_ref, w1_ref, w2_ref,
                     ln3g_ref, ln3b_ref, out_ref):
    subs = [slice(r, r + ATTN_SUB) for r in range(0, x_ref.shape[0], ATTN_SUB)]
    x2s = _attn_part(subs, x_ref, mk_ref, mv_ref, wq_ref, wo_ref, ln2g_ref, ln2b_ref)
    _mlp_part(subs, x2s, w1_ref, w2_ref, ln3g_ref, ln3b_ref, out_ref)


def _attn_mlp_call(x1, mk, mv, wq, wo, ln2g, ln2b, w1, w2, ln3g, ln3b):
    B, L, _ = x1.shape
    M = mk.shape[1]
    tm = ATTN_BLOCK
    tok = pl.BlockSpec((None, tm, D_MODEL), lambda b, i: (b, i, 0))
    memb = pl.BlockSpec((None, M, D_MODEL), lambda b, i: (b, 0, 0))
    const = lambda a: pl.BlockSpec(a.shape, lambda b, i: (0,) * a.ndim, pipeline_mode=pl.Buffered(1))
    return pl.pallas_call(
        _attn_mlp_kernel,
        grid=(B, L // tm),
        in_specs=[tok, memb, memb, const(wq), const(wo), const(ln2g), const(ln2b), const(w1), const(w2),
                  const(ln3g), const(ln3b)],
        out_specs=tok,
        out_shape=jax.ShapeDtypeStruct((B, L, D_MODEL), F32),
        compiler_params=pltpu.CompilerParams(dimension_semantics=("parallel", "parallel"),
                                             vmem_limit_bytes=VMEM_LIMIT_BYTES),
        name="attn_mlp",
    )(x1, mk, mv, wq, wo, ln2g, ln2b, w1, w2, ln3g, ln3b)


def _prepare(w_in, conv_w, conv_b, a_log_f, a_log_b, dt_bias_f, dt_bias_b, d_skip, ssd_norm_g,
             w_gk_f, b_gk_f, w_gk_b, b_gk_b, gla_norm_g, w_out, ln1_g, ln1_b,
             w_mq, w_mk, w_mv, w_mo, ln2_g, ln2_b, w_ff1, w_ff2, ln3_g, ln3_b):
    offs = [0]
    for s in IN_SIZES:
        offs.append(offs[-1] + s)
    col = lambda i: w_in[:, offs[i]:offs[i + 1]]
    wz = col(0).astype(BF16)
    wxbc = jnp.concatenate([col(1), col(2), col(3)], axis=1).astype(BF16)
    wq, wk, wv, wg = (col(i).astype(BF16) for i in (6, 7, 8, 9))
    pad = LANES - 2 * SSD_HEADS - GLA_LOWRANK
    wsm = jnp.concatenate([col(4), col(5), col(10), jnp.zeros((D_MODEL, pad), F32)], axis=1).astype(BF16)
    row = lambda a: a.reshape(1, -1).astype(F32)
    zpad = jnp.zeros((1, LANES - 2 * SSD_HEADS), F32)
    dtbias = jnp.concatenate([row(dt_bias_f), row(dt_bias_b), zpad], axis=1)
    alog = jnp.concatenate([row(a_log_f), row(a_log_b), zpad], axis=1)

    def gk_pad(w):
        return jnp.zeros((LANES, GLA_KEY), F32).at[SM_LR:SM_LR + GLA_LOWRANK].set(w).astype(BF16)

    def tri3(qn, reverse):
        r = jnp.arange(qn)[:, None]
        c = jnp.arange(qn)[None, :]
        t = (c >= r) if reverse else (r >= c)
        return jnp.tile(t.astype(BF16), (1, 3))

    def e2(lane_off):
        r = jnp.arange(LANES)[:, None]
        c = jnp.arange(SSD_WIDTH)[None, :]
        e = (c // SSD_HEADDIM == r - lane_off).astype(BF16)
        return jnp.concatenate([e, e], axis=0)

    return dict(
        proj=(wz, wxbc, wq, wk, wv, wg, wsm, conv_w.astype(F32), row(conv_b), dtbias),
        bwd=(alog, gk_pad(w_gk_b), row(b_gk_b), tri3(SSD_CHUNK, True), tri3(GLA_CHUNK, True), e2(SM_DTB)),
        fwd=(alog, gk_pad(w_gk_f), row(b_gk_f), tri3(SSD_CHUNK, False), tri3(GLA_CHUNK, False), e2(SM_DTF)),
        comb=(row(jnp.repeat(d_skip, SSD_HEADDIM)), row(ssd_norm_g), row(gla_norm_g), w_out.astype(BF16),
              row(ln1_g), row(ln1_b)),
        memkv=(w_mk.astype(BF16), w_mv.astype(BF16)),
        attn=(w_mq.astype(BF16), w_mo.astype(BF16), row(ln2_g), row(ln2_b), w_ff1.astype(BF16),
              w_ff2.astype(BF16), row(ln3_g), row(ln3_b)),
    )


def _encoder_layer(x, mem, p):
    z, xs, bc, q, k, v, g, sm = _proj_call(x, *p["proj"])
    yb, ob = _scan_bwd_call(sm, xs, bc, q, k, v, *p["bwd"])
    x1 = _scan_fwd_call(sm, xs, bc, q, k, v, *p["fwd"], x, z, g, yb, ob, *p["comb"])
    mk, mv = _memkv_call(mem, *p["memkv"])
    return _attn_mlp_call(x1, mk, mv, *p["attn"])


def kernel(x_prompt, x_sample, mem_prompt, mem_sample, w_in, conv_w, conv_b, a_log_f, a_log_b, dt_bias_f,
           dt_bias_b, d_skip, ssd_norm_g, w_gk_f, b_gk_f, w_gk_b, b_gk_b, gla_norm_g, w_out, ln1_g, ln1_b,
           w_mq, w_mk, w_mv, w_mo, ln2_g, ln2_b, w_ff1, w_ff2, ln3_g, ln3_b):
    params = (w_in, conv_w, conv_b, a_log_f, a_log_b, dt_bias_f, dt_bias_b, d_skip, ssd_norm_g,
              w_gk_f, b_gk_f, w_gk_b, b_gk_b, gla_norm_g, w_out, ln1_g, ln1_b,
              w_mq, w_mk, w_mv, w_mo, ln2_g, ln2_b, w_ff1, w_ff2, ln3_g, ln3_b)
    assert all(a.shape[0] == DEPTH for a in params)
    p = _prepare(*[a[0] for a in params])
    return (_encoder_layer(x_prompt, mem_prompt, p), _encoder_layer(x_sample, mem_sample, p))
```

```python
import jax
import jax.numpy as jnp
from jax import lax
from jax.experimental import pallas as pl
from jax.experimental.pallas import tpu as pltpu

F32 = jnp.float32
BF16 = jnp.bfloat16

D_MODEL = 1024
SSD_WIDTH = 1024
SSD_HEADDIM = 64
SSD_HEADS = 16
SSD_GROUPS = 2
SSD_HPG = 8
SSD_STATE = 128
SSD_CHUNK = 128
SSD_GW = SSD_HPG * SSD_HEADDIM
CONV_K = 5
CONV_CH = SSD_WIDTH + 2 * SSD_GROUPS * SSD_STATE
BC_WIDTH = 2 * SSD_GROUPS * SSD_STATE
GLA_HEADS = 4
GLA_KEY = 512
GLA_VAL = 1024
GLA_DK = 128
GLA_DV = 256
GLA_LOWRANK = 16
GLA_GATE_NORM = 16.0
GLA_CHUNK = 64
MIX_WIDTH = SSD_WIDTH + GLA_VAL
IN_SIZES = (SSD_WIDTH, SSD_WIDTH, SSD_GROUPS * SSD_STATE, SSD_GROUPS * SSD_STATE, SSD_HEADS, SSD_HEADS,
            GLA_KEY, GLA_KEY, GLA_VAL, GLA_VAL, GLA_LOWRANK)
MEM_HEADS = 4
MEM_HD = 256
D_FF = 4096
DEPTH = 1
ALPHA = (2 * DEPTH) ** 0.25
LN_EPS = 1e-5
RMS_EPS = 1e-5

LANES = 128
SUBLANES = 8
VMEM_LIMIT_BYTES = 56 * 1024 * 1024

SM_DTF = 0
SM_DTB = SSD_HEADS
SM_LR = 2 * SSD_HEADS
SM_USED_DT = 2 * SSD_HEADS

HALO = SUBLANES
CONV_L = (CONV_K - 1) // 2

PROJ_BLOCK = 512
CONV_COLS = 512
SCAN_BLOCK = 512
ATTN_BLOCK = 1024
ATTN_SUB = 256
FF_CHUNK = 1024


def _dot(a, b):
    return jnp.dot(a, b, preferred_element_type=F32)


def _dot_nt(a, b):
    return lax.dot_general(a, b, (((1,), (1,)), ((), ())), preferred_element_type=F32)


def _dot_tn(a, b):
    return lax.dot_general(a, b, (((0,), (0,)), ((), ())), preferred_element_type=F32)


def _sigmoid(x):
    return 1.0 / (1.0 + jnp.exp(-x))


def _silu(x):
    return x * _sigmoid(x)


def _softplus(x):
    return jnp.maximum(x, 0.0) + jnp.log1p(jnp.exp(-jnp.abs(x)))


def _split3(a):
    hi = a.astype(BF16)
    r = a - hi.astype(F32)
    mid = r.astype(BF16)
    lo = (r - mid.astype(F32)).astype(BF16)
    return hi, mid, lo


def _cumsum_rows(tri3, a):
    hi, mid, lo = _split3(a)
    return _dot(tri3, jnp.concatenate([hi, mid, lo], axis=0))


def _expand_heads(a, e2):
    hi = a.astype(BF16)
    lo = (a - hi.astype(F32)).astype(BF16)
    return _dot(jnp.concatenate([hi, lo], axis=1), e2)


def _layer_norm(h, g, b):
    mu = jnp.mean(h, axis=-1, keepdims=True)
    d = h - mu
    var = jnp.mean(d * d, axis=-1, keepdims=True)
    return d * lax.rsqrt(var + LN_EPS) * g + b


def _proj_kernel(x_ref, xp_ref, xn_ref, wz_ref, wxbc_ref, wq_ref, wk_ref, wv_ref, wg_ref, wsm_ref,
                 convw_ref, convb_ref, dtbias_ref,
                 z_ref, xs_ref, bc_ref, q_ref, k_ref, v_ref, g_ref, sm_ref, ext_ref):
    i = pl.program_id(1)
    nb = pl.num_programs(1)
    tm = x_ref.shape[0]
    xb = x_ref[...].astype(BF16)
    xpb = xp_ref[...].astype(BF16)
    xnb = xn_ref[...].astype(BF16)
    has_prev = i > 0
    has_next = i < nb - 1
    cw = CONV_COLS
    for c in range(0, CONV_CH, cw):
        w = wxbc_ref[:, c:c + cw]
        up = jnp.where(has_prev, _dot(xpb, w), 0.0)
        uc = _dot(xb, w)
        un = jnp.where(has_next, _dot(xnb, w), 0.0)
        for t in range(cw // LANES):
            ls = slice(t * LANES, (t + 1) * LANES)
            ext_ref[c // LANES + t, 0:HALO, :] = up[:, ls]
            ext_ref[c // LANES + t, HALO:HALO + tm, :] = uc[:, ls]
            ext_ref[c // LANES + t, HALO + tm:HALO + tm + HALO, :] = un[:, ls]

    def conv_tile(ct):
        ls = slice(ct * LANES, (ct + 1) * LANES)
        acc = jnp.broadcast_to(convb_ref[:, ls], (tm, LANES))
        for j in range(CONV_K):
            r0 = HALO - CONV_L + j
            acc = acc + convw_ref[j:j + 1, ls] * ext_ref[ct, r0:r0 + tm, :]
        return _silu(acc)

    n_xs = SSD_WIDTH // LANES
    for ct in range(n_xs):
        xs_ref[:, ct * LANES:(ct + 1) * LANES] = conv_tile(ct)
    for ct in range(n_xs, CONV_CH // LANES):
        bc_ref[:, (ct - n_xs) * LANES:(ct - n_xs + 1) * LANES] = conv_tile(ct).astype(BF16)

    z_ref[...] = _dot(xb, wz_ref[...])
    q_ref[...] = _dot(xb, wq_ref[...]) * (GLA_DK ** -0.5)
    k_ref[...] = _dot(xb, wk_ref[...])
    v_ref[...] = _dot(xb, wv_ref[...]).astype(BF16)
    g_ref[...] = _dot(xb, wg_ref[...])
    sm = _dot(xb, wsm_ref[...])
    lane = lax.broadcasted_iota(jnp.int32, sm.shape, 1)
    sm_ref[...] = jnp.where(lane < SM_USED_DT, _softplus(sm + dtbias_ref[...]), sm)


def _proj_call(x, wz, wxbc, wq, wk, wv, wg, wsm, convw, convb, dtbias):
    B, L, _ = x.shape
    tm = PROJ_BLOCK
    nb = L // tm
    hb = tm // HALO
    const = lambda shape: pl.BlockSpec(shape, lambda b, i: (0,) * len(shape), pipeline_mode=pl.Buffered(1))
    tok = lambda w: pl.BlockSpec((None, tm, w), lambda b, i: (b, i, 0))
    outs = ((SSD_WIDTH, F32), (SSD_WIDTH, F32), (BC_WIDTH, BF16), (GLA_KEY, F32), (GLA_KEY, F32), (GLA_VAL, BF16),
            (GLA_VAL, F32), (LANES, F32))
    return pl.pallas_call(
        _proj_kernel,
        grid=(B, nb),
        in_specs=[
            tok(D_MODEL),
            pl.BlockSpec((None, HALO, D_MODEL), lambda b, i: (b, jnp.maximum(i * hb - 1, 0), 0)),
            pl.BlockSpec((None, HALO, D_MODEL), lambda b, i: (b, jnp.minimum((i + 1) * hb, L // HALO - 1), 0)),
            const(wz.shape), const(wxbc.shape), const(wq.shape), const(wk.shape), const(wv.shape),
            const(wg.shape), const(wsm.shape), const(convw.shape), const(convb.shape), const(dtbias.shape),
        ],
        out_specs=[tok(w) for w, _ in outs],
        out_shape=[jax.ShapeDtypeStruct((B, L, w), dt) for w, dt in outs],
        scratch_shapes=[pltpu.VMEM((CONV_CH // LANES, tm + 2 * HALO, LANES), F32)],
        compiler_params=pltpu.CompilerParams(dimension_semantics=("parallel", "parallel"),
                                             vmem_limit_bytes=VMEM_LIMIT_BYTES),
        name="proj",
    )(x, x, x, wz, wxbc, wq, wk, wv, wg, wsm, convw, convb, dtbias)


def _ssd_prep(reverse, r0, sm_ref, xs_ref, alog_ref, tri_ref, e2_ref):
    q = SSD_CHUNK
    rows = slice(r0, r0 + q)
    sm = sm_ref[rows, :]
    lane1 = lax.broadcasted_iota(jnp.int32, (1, LANES), 1)
    a_neg = jnp.where(lane1 < SM_USED_DT, -jnp.exp(alog_ref[...]), 0.0)
    acs = _cumsum_rows(tri_ref[...], sm * a_neg)
    last = acs[0:1, :] if reverse else acs[q - 1:q, :]
    dte = jnp.exp(last - acs)
    e2 = e2_ref[...]
    dt_x = _expand_heads(sm, e2)
    dtd_x = _expand_heads(sm * dte, e2)
    eacs_x = _expand_heads(jnp.exp(acs), e2)
    cdec_x = _expand_heads(jnp.broadcast_to(jnp.exp(last), (SUBLANES, LANES)), e2)[0:1, :]
    xs = xs_ref[rows, :]
    xdt = xs * dt_x
    lane_w = lax.broadcasted_iota(jnp.int32, (q, SSD_WIDTH), 1)
    even = (lane_w & SSD_HEADDIM) == 0
    xdt_e = jnp.where(even, xdt, 0.0).astype(BF16)
    xdt_o = jnp.where(even, 0.0, xdt).astype(BF16)
    xdd = (xs * dtd_x).astype(BF16)
    return dict(acs=acs, acs_t=acs.T, xdt_e=xdt_e, xdt_o=xdt_o, xdd=xdd, eacs_x=eacs_x, cdec_x=cdec_x)


def _ssd_intra(reverse, r0, p, bc_ref):
    q = SSD_CHUNK
    rows = slice(r0, r0 + q)
    lane_off = SM_DTB if reverse else SM_DTF
    li = lax.broadcasted_iota(jnp.int32, (q, q), 0)
    si = lax.broadcasted_iota(jnp.int32, (q, q), 1)
    mask = (si >= li) if reverse else (li >= si)
    acs, acs_t = p["acs"], p["acs_t"]
    ys, cgs, snews = [], [], []
    for g in range(SSD_GROUPS):
        b0 = g * SSD_STATE
        c0 = SSD_GROUPS * SSD_STATE + g * SSD_STATE
        bg = bc_ref[rows, b0:b0 + SSD_STATE]
        cg = bc_ref[rows, c0:c0 + SSD_STATE]
        cb = _dot_nt(cg, bg)
        for pr in range(SSD_GW // LANES):
            pl0 = g * SSD_GW + pr * LANES
            ms = []
            for hh in range(LANES // SSD_HEADDIM):
                col = lane_off + g * SSD_HPG + pr * (LANES // SSD_HEADDIM) + hh
                seg = acs[:, col:col + 1] - acs_t[col:col + 1, :]
                lm = jnp.exp(jnp.where(mask, seg, -jnp.inf))
                ms.append((cb * lm).astype(BF16))
            rhs = jnp.concatenate([p["xdt_e"][:, pl0:pl0 + LANES], p["xdt_o"][:, pl0:pl0 + LANES]], axis=0)
            ys.append(_dot(jnp.concatenate(ms, axis=1), rhs))
        gl = slice(g * SSD_GW, (g + 1) * SSD_GW)
        cgs.append(cg)
        snews.append(_dot_tn(bg, p["xdd"][:, gl]))
    return ys, cgs, snews


def _ssd_carry(p, ys, cgs, snews, states):
    outs, new_states = [], []
    for g in range(SSD_GROUPS):
        gl = slice(g * SSD_GW, (g + 1) * SSD_GW)
        s_in = states[g]
        y_off = _dot(cgs[g], s_in.astype(BF16)) * p["eacs_x"][:, gl]
        npr = SSD_GW // LANES
        for pr in range(npr):
            outs.append(ys[g * npr + pr] + y_off[:, pr * LANES:(pr + 1) * LANES])
        new_states.append(s_in * p["cdec_x"][:, gl] + snews[g])
    return jnp.concatenate(outs, axis=1), new_states


def _gla_prep(reverse, sm_ref, q_ref, k_ref, wgk_ref, bgk_ref, tri_ref):
    tb = sm_ref.shape[0]
    q = GLA_CHUNK
    u = _dot(sm_ref[...].astype(BF16), wgk_ref[...]) + bgk_ref[...]
    lg = (jnp.minimum(u, 0.0) - jnp.log(1.0 + jnp.exp(-jnp.abs(u)))) / GLA_GATE_NORM
    hi, mid, lo = _split3(lg)
    tri = tri_ref[...]
    bcs_c, last_c = [], []
    for r0 in range(0, tb, q):
        rows = slice(r0, r0 + q)
        b = _dot(tri, jnp.concatenate([hi[rows], mid[rows], lo[rows]], axis=0))
        bcs_c.append(b)
        last_c.append(b[0:1, :] if reverse else b[q - 1:q, :])
    bcs = jnp.concatenate(bcs_c, axis=0)
    last_b = jnp.concatenate([jnp.broadcast_to(l, (q, GLA_KEY)) for l in last_c], axis=0)
    kk = k_ref[...]
    qt = (q_ref[...] * jnp.exp(bcs)).astype(BF16)
    kt = (kk * jnp.exp(-bcs)).astype(BF16)
    ke = (kk * jnp.exp(last_b - bcs)).astype(BF16)
    dec_cols = [[jnp.broadcast_to(jnp.exp(l[:, h * GLA_DK:(h + 1) * GLA_DK]), (GLA_DK, GLA_DK)).T
                 for h in range(GLA_HEADS)] for l in last_c]
    return qt, kt, ke, dec_cols


def _gla_intra(reverse, r0, qt, kt, ke, v_ref):
    q = GLA_CHUNK
    rows = slice(r0, r0 + q)
    li = lax.broadcasted_iota(jnp.int32, (q, q), 0)
    si = lax.broadcasted_iota(jnp.int32, (q, q), 1)
    mask = (si >= li) if reverse else (li >= si)
    atts, ups = [], []
    for h in range(GLA_HEADS):
        kl = slice(h * GLA_DK, (h + 1) * GLA_DK)
        vl = slice(h * GLA_DV, (h + 1) * GLA_DV)
        atts.append(jnp.where(mask, _dot_nt(qt[rows, kl], kt[rows, kl]), 0.0).astype(BF16))
        ups.append(_dot_tn(ke[rows, kl], v_ref[rows, vl]))
    return atts, ups


def _gla_carry(r0, qt, dec_cols, atts, ups, v_ref, states):
    rows = slice(r0, r0 + GLA_CHUNK)
    outs, new_states = [], []
    for h in range(GLA_HEADS):
        kl = slice(h * GLA_DK, (h + 1) * GLA_DK)
        vl = slice(h * GLA_DV, (h + 1) * GLA_DV)
        st = states[h]
        lhs = jnp.concatenate([qt[rows, kl], atts[h]], axis=1)
        rhs = jnp.concatenate([st.astype(BF16), v_ref[rows, vl]], axis=0)
        outs.append(_dot(lhs, rhs))
        dcol = dec_cols[h]
        new_states.append(st * jnp.concatenate([dcol, dcol], axis=1) + ups[h])
    return jnp.concatenate(outs, axis=1), new_states


def _run_scans(reverse, sm_ref, xs_ref, bc_ref, q_ref, k_ref, v_ref, alog_ref, wgk_ref, bgk_ref,
               tri_s_ref, tri_g_ref, e2_ref, ssd_state, gla_state):
    tb = sm_ref.shape[0]

    @pl.when(pl.program_id(1) == 0)
    def _():
        ssd_state[...] = jnp.zeros_like(ssd_state)
        gla_state[...] = jnp.zeros_like(gla_state)

    s_states = [ssd_state[g] for g in range(SSD_GROUPS)]
    g_states = [gla_state[h] for h in range(GLA_HEADS)]
    s_chunks = list(range(0, tb, SSD_CHUNK))
    g_chunks = list(range(0, tb, GLA_CHUNK))
    preps = [_ssd_prep(reverse, r0, sm_ref, xs_ref, alog_ref, tri_s_ref, e2_ref) for r0 in s_chunks]
    qt, kt, ke, dec_cols = _gla_prep(reverse, sm_ref, q_ref, k_ref, wgk_ref, bgk_ref, tri_g_ref)
    s_intra = [_ssd_intra(reverse, r0, p, bc_ref) for r0, p in zip(s_chunks, preps)]
    g_intra = [_gla_intra(reverse, r0, qt, kt, ke, v_ref) for r0 in g_chunks]
    s_order = range(len(s_chunks) - 1, -1, -1) if reverse else range(len(s_chunks))
    g_order = range(len(g_chunks) - 1, -1, -1) if reverse else range(len(g_chunks))
    y_c = [None] * len(s_chunks)
    o_c = [None] * len(g_chunks)
    for c in s_order:
        y_c[c], s_states = _ssd_carry(preps[c], *s_intra[c], s_states)
    for c in g_order:
        o_c[c], g_states = _gla_carry(g_chunks[c], qt, dec_cols[c], *g_intra[c], v_ref, g_states)
    return jnp.concatenate(y_c, axis=0), jnp.concatenate(o_c, axis=0), s_states, g_states


def _store_states(ssd_state, gla_state, s_states, g_states):
    for g in range(SSD_GROUPS):
        ssd_state[g] = s_states[g]
    for h in range(GLA_HEADS):
        gla_state[h] = g_states[h]


def _scan_bwd_kernel(sm_ref, xs_ref, bc_ref, q_ref, k_ref, v_ref, alog_ref, wgk_ref, bgk_ref,
                     tri_s_ref, tri_g_ref, e2_ref, y_ref, o_ref, ssd_state, gla_state):
    y, o, s_states, g_states = _run_scans(True, sm_ref, xs_ref, bc_ref, q_ref, k_ref, v_ref, alog_ref, wgk_ref,
                                          bgk_ref, tri_s_ref, tri_g_ref, e2_ref, ssd_state, gla_state)
    y_ref[...] = y
    o_ref[...] = o
    _store_states(ssd_state, gla_state, s_states, g_states)


def _scan_fwd_kernel(sm_ref, xs_ref, bc_ref, q_ref, k_ref, v_ref, alog_ref, wgk_ref, bgk_ref,
                     tri_s_ref, tri_g_ref, e2_ref,
                     x_ref, z_ref, g_ref, yb_ref, ob_ref, dskip_ref, ssdg_ref, glag_ref, wout_ref,
                     ln_g_ref, ln_b_ref,
                     x1_ref, ssd_state, gla_state):
    yf, of, s_states, g_states = _run_scans(False, sm_ref, xs_ref, bc_ref, q_ref, k_ref, v_ref, alog_ref, wgk_ref,
                                            bgk_ref, tri_s_ref, tri_g_ref, e2_ref, ssd_state, gla_state)
    acc = ALPHA * x_ref[...]
    gw = SSD_WIDTH // SSD_GROUPS
    for g in range(SSD_GROUPS):
        sl = slice(g * gw, (g + 1) * gw)
        y = yf[:, sl] + yb_ref[:, sl] + dskip_ref[:, sl] * xs_ref[:, sl]
        y = y * _silu(z_ref[:, sl])
        ms = jnp.mean(y * y, axis=-1, keepdims=True)
        yn = y * lax.rsqrt(ms + RMS_EPS) * ssdg_ref[:, sl]
        acc = acc + _dot(yn.astype(BF16), wout_ref[sl, :])
    for h in range(GLA_HEADS):
        sl = slice(h * GLA_DV, (h + 1) * GLA_DV)
        o = of[:, sl] + ob_ref[:, sl]
        ms = jnp.mean(o * o, axis=-1, keepdims=True)
        on = o * lax.rsqrt(ms + RMS_EPS) * glag_ref[...] * _silu(g_ref[:, sl])
        acc = acc + _dot(on.astype(BF16), wout_ref[SSD_WIDTH + h * GLA_DV:SSD_WIDTH + (h + 1) * GLA_DV, :])
    x1_ref[...] = _layer_norm(acc, ln_g_ref[...], ln_b_ref[...])
    _store_states(ssd_state, gla_state, s_states, g_states)


def _scan_common_specs(L, tb, reverse, consts):
    nb = L // tb
    if reverse:
        tok = lambda w: pl.BlockSpec((None, tb, w), lambda b, i: (b, nb - 1 - i, 0))
    else:
        tok = lambda w: pl.BlockSpec((None, tb, w), lambda b, i: (b, i, 0))
    const = lambda a: pl.BlockSpec(a.shape, lambda b, i: (0,) * a.ndim, pipeline_mode=pl.Buffered(1))
    return tok, [const(a) for a in consts]


def _scan_bwd_call(sm, xs, bc, q, k, v, alog, wgk, bgk, tri_s, tri_g, e2):
    B, L, _ = sm.shape
    tb = SCAN_BLOCK
    consts = (alog, wgk, bgk, tri_s, tri_g, e2)
    tok, const_specs = _scan_common_specs(L, tb, True, consts)
    return pl.pallas_call(
        _scan_bwd_kernel,
        grid=(B, L // tb),
        in_specs=[tok(LANES), tok(SSD_WIDTH), tok(BC_WIDTH), tok(GLA_KEY), tok(GLA_KEY), tok(GLA_VAL)] + const_specs,
        out_specs=[tok(SSD_WIDTH), tok(GLA_VAL)],
        out_shape=[jax.ShapeDtypeStruct((B, L, SSD_WIDTH), F32), jax.ShapeDtypeStruct((B, L, GLA_VAL), F32)],
        scratch_shapes=[pltpu.VMEM((SSD_GROUPS, SSD_STATE, SSD_GW), F32),
                        pltpu.VMEM((GLA_HEADS, GLA_DK, GLA_DV), F32)],
        compiler_params=pltpu.CompilerParams(dimension_semantics=("arbitrary", "arbitrary"),
                                             vmem_limit_bytes=VMEM_LIMIT_BYTES),
        name="scan_bwd",
    )(sm, xs, bc, q, k, v, *consts)


def _scan_fwd_call(sm, xs, bc, q, k, v, alog, wgk, bgk, tri_s, tri_g, e2,
                   x, z, g, yb, ob, dskip, ssdg, glag, wout, ln_g, ln_b):
    B, L, _ = sm.shape
    tb = SCAN_BLOCK
    consts_a = (alog, wgk, bgk, tri_s, tri_g, e2)
    consts_b = (dskip, ssdg, glag, wout, ln_g, ln_b)
    tok, const_a = _scan_common_specs(L, tb, False, consts_a)
    _, const_b = _scan_common_specs(L, tb, False, consts_b)
    return pl.pallas_call(
        _scan_fwd_kernel,
        grid=(B, L // tb),
        in_specs=([tok(LANES), tok(SSD_WIDTH), tok(BC_WIDTH), tok(GLA_KEY), tok(GLA_KEY), tok(GLA_VAL)] + const_a
                  + [tok(D_MODEL), tok(SSD_WIDTH), tok(GLA_VAL), tok(SSD_WIDTH), tok(GLA_VAL)] + const_b),
        out_specs=tok(D_MODEL),
        out_shape=jax.ShapeDtypeStruct((B, L, D_MODEL), F32),
        scratch_shapes=[pltpu.VMEM((SSD_GROUPS, SSD_STATE, SSD_GW), F32),
                        pltpu.VMEM((GLA_HEADS, GLA_DK, GLA_DV), F32)],
        compiler_params=pltpu.CompilerParams(dimension_semantics=("arbitrary", "arbitrary"),
                                             vmem_limit_bytes=VMEM_LIMIT_BYTES),
        name="scan_fwd",
    )(sm, xs, bc, q, k, v, *consts_a, x, z, g, yb, ob, *consts_b)


def _memkv_kernel(mem_ref, wk_ref, wv_ref, k_ref, v_ref):
    m = mem_ref[...].astype(BF16)
    k_ref[...] = _dot(m, wk_ref[...]).astype(BF16)
    v_ref[...] = _dot(m, wv_ref[...]).astype(BF16)


def _memkv_call(mem, wk, wv):
    B, M, _ = mem.shape
    blk = pl.BlockSpec((None, M, D_MODEL), lambda b: (b, 0, 0))
    const = lambda a: pl.BlockSpec(a.shape, lambda b: (0,) * a.ndim, pipeline_mode=pl.Buffered(1))
    return pl.pallas_call(
        _memkv_kernel,
        grid=(B,),
        in_specs=[blk, const(wk), const(wv)],
        out_specs=[blk, blk],
        out_shape=[jax.ShapeDtypeStruct((B, M, D_MODEL), BF16)] * 2,
        compiler_params=pltpu.CompilerParams(dimension_semantics=("parallel",),
                                             vmem_limit_bytes=VMEM_LIMIT_BYTES),
        name="memkv",
    )(mem, wk, wv)


def _attn_part(subs, x_ref, mk_ref, mv_ref, wq_ref, wo_ref, ln2g_ref, ln2b_ref):
    x1s = [x_ref[rows, :] for rows in subs]
    qfs = [_dot(x1.astype(BF16), wq_ref[...]) for x1 in x1s]
    accs = [ALPHA * x1 for x1 in x1s]
    for h in range(MEM_HEADS):
        sl = slice(h * MEM_HD, (h + 1) * MEM_HD)
        ss = [_dot_nt(qf[:, sl].astype(BF16), mk_ref[:, sl]) * (MEM_HD ** -0.5) for qf in qfs]
        ps = []
        for s in ss:
            e = jnp.exp(s - jnp.max(s, axis=-1, keepdims=True))
            ps.append((e / jnp.sum(e, axis=-1, keepdims=True)).astype(BF16))
        ohs = [_dot(p, mv_ref[:, sl]).astype(BF16) for p in ps]
        accs = [acc + _dot(oh, wo_ref[sl, :]) for acc, oh in zip(accs, ohs)]
    return [_layer_norm(acc, ln2g_ref[...], ln2b_ref[...]) for acc in accs]


def _mlp_part(subs, x2s, w1_ref, w2_ref, ln3g_ref, ln3b_ref, out_ref):
    x2bs = [x2.astype(BF16) for x2 in x2s]
    accs = [ALPHA * x2 for x2 in x2s]
    for c in range(0, D_FF, FF_CHUNK):
        hs = [jnp.maximum(_dot(x2b, w1_ref[:, c:c + FF_CHUNK]), 0.0) for x2b in x2bs]
        accs = [acc + _dot((hdn * hdn).astype(BF16), w2_ref[c:c + FF_CHUNK, :]) for acc, hdn in zip(accs, hs)]
    for rows, acc in zip(subs, accs):
        out_ref[rows, :] = _layer_norm(acc, ln3g_ref[...], ln3b_ref[...])


def _attn_mlp_kernel(x_ref, mk_ref, mv_ref, wq_ref, wo_ref, ln2g_ref, ln2b_ref, w1_ref, w2_ref,
                     ln3g_ref, ln3b_ref, out_ref):
    subs = [slice(r, r + ATTN_SUB) for r in range(0, x_ref.shape[0], ATTN_SUB)]
    x2s = _attn_part(subs, x_ref, mk_ref, mv_ref, wq_ref, wo_ref, ln2g_ref, ln2b_ref)
    _mlp_part(subs, x2s, w1_ref, w2_ref, ln3g_ref, ln3b_ref, out_ref)


def _attn_mlp_call(x1, mk, mv, wq, wo, ln2g, ln2b, w1, w2, ln3g, ln3b):
    B, L, _ = x1.shape
    M = mk.shape[1]
    tm = ATTN_BLOCK
    tok = pl.BlockSpec((None, tm, D_MODEL), lambda b, i: (b, i, 0))
    memb = pl.BlockSpec((None, M, D_MODEL), lambda b, i: (b, 0, 0))
    const = lambda a: pl.BlockSpec(a.shape, lambda b, i: (0,) * a.ndim, pipeline_mode=pl.Buffered(1))
    return pl.pallas_call(
        _attn_mlp_kernel,
        grid=(B, L // tm),
        in_specs=[tok, memb, memb, const(wq), const(wo), const(ln2g), const(ln2b), const(w1), const(w2),
                  const(ln3g), const(ln3b)],
        out_specs=tok,
        out_shape=jax.ShapeDtypeStruct((B, L, D_MODEL), F32),
        compiler_params=pltpu.CompilerParams(dimension_semantics=("parallel", "parallel"),
                                             vmem_limit_bytes=VMEM_LIMIT_BYTES),
        name="attn_mlp",
    )(x1, mk, mv, wq, wo, ln2g, ln2b, w1, w2, ln3g, ln3b)


def _prepare(w_in, conv_w, conv_b, a_log_f, a_log_b, dt_bias_f, dt_bias_b, d_skip, ssd_norm_g,
             w_gk_f, b_gk_f, w_gk_b, b_gk_b, gla_norm_g, w_out, ln1_g, ln1_b,
             w_mq, w_mk, w_mv, w_mo, ln2_g, ln2_b, w_ff1, w_ff2, ln3_g, ln3_b):
    offs = [0]
    for s in IN_SIZES:
        offs.append(offs[-1] + s)
    col = lambda i: w_in[:, offs[i]:offs[i + 1]]
    wz = col(0).astype(BF16)
    wxbc = jnp.concatenate([col(1), col(2), col(3)], axis=1).astype(BF16)
    wq, wk, wv, wg = (col(i).astype(BF16) for i in (6, 7, 8, 9))
    pad = LANES - 2 * SSD_HEADS - GLA_LOWRANK
    wsm = jnp.concatenate([col(4), col(5), col(10), jnp.zeros((D_MODEL, pad), F32)], axis=1).astype(BF16)
    row = lambda a: a.reshape(1, -1).astype(F32)
    zpad = jnp.zeros((1, LANES - 2 * SSD_HEADS), F32)
    dtbias = jnp.concatenate([row(dt_bias_f), row(dt_bias_b), zpad], axis=1)
    alog = jnp.concatenate([row(a_log_f), row(a_log_b), zpad], axis=1)

    def gk_pad(w):
        return jnp.zeros((LANES, GLA_KEY), F32).at[SM_LR:SM_LR + GLA_LOWRANK].set(w).astype(BF16)

    def tri3(qn, reverse):
        r = jnp.arange(qn)[:, None]
        c = jnp.arange(qn)[None, :]
        t = (c >= r) if reverse else (r >= c)
        return jnp.tile(t.astype(BF16), (1, 3))

    def e2(lane_off):
        r = jnp.arange(LANES)[:, None]
        c = jnp.arange(SSD_WIDTH)[None, :]
        e = (c // SSD_HEADDIM == r - lane_off).astype(BF16)
        return jnp.concatenate([e, e], axis=0)

    return dict(
        proj=(wz, wxbc, wq, wk, wv, wg, wsm, conv_w.astype(F32), row(conv_b), dtbias),
        bwd=(alog, gk_pad(w_gk_b), row(b_gk_b), tri3(SSD_CHUNK, True), tri3(GLA_CHUNK, True), e2(SM_DTB)),
        fwd=(alog, gk_pad(w_gk_f), row(b_gk_f), tri3(SSD_CHUNK, False), tri3(GLA_CHUNK, False), e2(SM_DTF)),
        comb=(row(jnp.repeat(d_skip, SSD_HEADDIM)), row(ssd_norm_g), row(gla_norm_g), w_out.astype(BF16),
              row(ln1_g), row(ln1_b)),
        memkv=(w_mk.astype(BF16), w_mv.astype(BF16)),
        attn=(w_mq.astype(BF16), w_mo.astype(BF16), row(ln2_g), row(ln2_b), w_ff1.astype(BF16),
              w_ff2.astype(BF16), row(ln3_g), row(ln3_b)),
    )


def _encoder_layer(x, mem, p):
    z, xs, bc, q, k, v, g, sm = _proj_call(x, *p["proj"])
    yb, ob = _scan_bwd_call(sm, xs, bc, q, k, v, *p["bwd"])
    x1 = _scan_fwd_call(sm, xs, bc, q, k, v, *p["fwd"], x, z, g, yb, ob, *p["comb"])
    mk, mv = _memkv_call(mem, *p["memkv"])
    return _attn_mlp_call(x1, mk, mv, *p["attn"])


def kernel(x_prompt, x_sample, mem_prompt, mem_sample, w_in, conv_w, conv_b, a_log_f, a_log_b, dt_bias_f,
           dt_bias_b, d_skip, ssd_norm_g, w_gk_f, b_gk_f, w_gk_b, b_gk_b, gla_norm_g, w_out, ln1_g, ln1_b,
           w_mq, w_mk, w_mv, w_mo, ln2_g, ln2_b, w_ff1, w_ff2, ln3_g, ln3_b):
    params = (w_in, conv_w, conv_b, a_log_f, a_log_b, dt_bias_f, dt_bias_b, d_skip, ssd_norm_g,
              w_gk_f, b_gk_f, w_gk_b, b_gk_b, gla_norm_g, w_out, ln1_g, ln1_b,
              w_mq, w_mk, w_mv, w_mo, ln2_g, ln2_b, w_ff1, w_ff2, ln3_g, ln3_b)
    assert all(a.shape[0] == DEPTH for a in params)
    p = _prepare(*[a[0] for a in params])
    return (_encoder_layer(x_prompt, mem_prompt, p), _encoder_layer(x_sample, mem_sample, p))
```

```python
import jax
import jax.numpy as jnp
from jax import lax
from jax.experimental import pallas as pl
from jax.experimental.pallas import tpu as pltpu

F32 = jnp.float32
BF16 = jnp.bfloat16

D_MODEL = 1024
SSD_WIDTH = 1024
SSD_HEADDIM = 64
SSD_HEADS = 16
SSD_GROUPS = 2
SSD_HPG = 8
SSD_STATE = 128
SSD_CHUNK = 128
SSD_GW = SSD_HPG * SSD_HEADDIM
CONV_K = 5
CONV_CH = SSD_WIDTH + 2 * SSD_GROUPS * SSD_STATE
BC_WIDTH = 2 * SSD_GROUPS * SSD_STATE
GLA_HEADS = 4
GLA_KEY = 512
GLA_VAL = 1024
GLA_DK = 128
GLA_DV = 256
GLA_LOWRANK = 16
GLA_GATE_NORM = 16.0
GLA_CHUNK = 64
MIX_WIDTH = SSD_WIDTH + GLA_VAL
IN_SIZES = (SSD_WIDTH, SSD_WIDTH, SSD_GROUPS * SSD_STATE, SSD_GROUPS * SSD_STATE, SSD_HEADS, SSD_HEADS,
            GLA_KEY, GLA_KEY, GLA_VAL, GLA_VAL, GLA_LOWRANK)
MEM_HEADS = 4
MEM_HD = 256
D_FF = 4096
DEPTH = 1
ALPHA = (2 * DEPTH) ** 0.25
LN_EPS = 1e-5
RMS_EPS = 1e-5
LOG2E = 1.4426950408889634

LANES = 128
SUBLANES = 8
VMEM_LIMIT_BYTES = 56 * 1024 * 1024

SM_DTF = 0
SM_DTB = SSD_HEADS
SM_LR = 2 * SSD_HEADS
SM_USED_DT = 2 * SSD_HEADS

HALO = SUBLANES
CONV_L = (CONV_K - 1) // 2

PROJ_BLOCK = 512
CONV_COLS = 512
SCAN_BLOCK = 512
ATTN_BLOCK = 1024
ATTN_SUB = 256
FF_CHUNK = 1024


def _dot(a, b):
    return jnp.dot(a, b, preferred_element_type=F32)


def _dot_nt(a, b):
    return lax.dot_general(a, b, (((1,), (1,)), ((), ())), preferred_element_type=F32)


def _dot_tn(a, b):
    return lax.dot_general(a, b, (((0,), (0,)), ((), ())), preferred_element_type=F32)


def _sigmoid(x):
    return 1.0 / (1.0 + jnp.exp(-x))


def _silu(x):
    return x * _sigmoid(x)


def _softplus(x):
    return jnp.maximum(x, 0.0) + jnp.log1p(jnp.exp(-jnp.abs(x)))


def _split3(a):
    hi = a.astype(BF16)
    r = a - hi.astype(F32)
    mid = r.astype(BF16)
    lo = (r - mid.astype(F32)).astype(BF16)
    return hi, mid, lo


def _cumsum_rows(tri3, a):
    hi, mid, lo = _split3(a)
    return _dot(tri3, jnp.concatenate([hi, mid, lo], axis=0))


def _expand_heads(a, e2):
    hi = a.astype(BF16)
    lo = (a - hi.astype(F32)).astype(BF16)
    return _dot(jnp.concatenate([hi, lo], axis=1), e2)


def _layer_norm(h, g, b):
    mu = jnp.mean(h, axis=-1, keepdims=True)
    d = h - mu
    var = jnp.mean(d * d, axis=-1, keepdims=True)
    return d * lax.rsqrt(var + LN_EPS) * g + b


def _proj_kernel(x_ref, xp_ref, xn_ref, wz_ref, wxbc_ref, wq_ref, wk_ref, wv_ref, wg_ref, wsm_ref,
                 convw_ref, convb_ref, dtbias_ref,
                 z_ref, xs_ref, bc_ref, q_ref, k_ref, v_ref, g_ref, sm_ref, ext_ref):
    i = pl.program_id(1)
    nb = pl.num_programs(1)
    tm = x_ref.shape[0]
    xb = x_ref[...].astype(BF16)
    has_prev = i > 0
    has_next = i < nb - 1
    x_ext = jnp.concatenate([xp_ref[...], x_ref[...], xn_ref[...]], axis=0).astype(BF16)
    row = lax.broadcasted_iota(jnp.int32, (tm + 2 * HALO, 1), 0)
    keep = jnp.logical_and(jnp.logical_or(row >= HALO, has_prev), jnp.logical_or(row < HALO + tm, has_next))
    cw = CONV_COLS
    for c in range(0, CONV_CH, cw):
        u = jnp.where(keep, _dot(x_ext, wxbc_ref[:, c:c + cw]), 0.0)
        for t in range(cw // LANES):
            ext_ref[c // LANES + t, :, :] = u[:, t * LANES:(t + 1) * LANES]

    def conv_tile(ct):
        ls = slice(ct * LANES, (ct + 1) * LANES)
        acc = jnp.broadcast_to(convb_ref[:, ls], (tm, LANES))
        for j in range(CONV_K):
            r0 = HALO - CONV_L + j
            acc = acc + convw_ref[j:j + 1, ls] * ext_ref[ct, r0:r0 + tm, :]
        return _silu(acc)

    n_xs = SSD_WIDTH // LANES
    for ct in range(n_xs):
        xs_ref[:, ct * LANES:(ct + 1) * LANES] = conv_tile(ct)
    for ct in range(n_xs, CONV_CH // LANES):
        bc_ref[:, (ct - n_xs) * LANES:(ct - n_xs + 1) * LANES] = conv_tile(ct).astype(BF16)

    z_ref[...] = _dot(xb, wz_ref[...])
    q_ref[...] = _dot(xb, wq_ref[...]) * (GLA_DK ** -0.5)
    k_ref[...] = _dot(xb, wk_ref[...])
    v_ref[...] = _dot(xb, wv_ref[...]).astype(BF16)
    g_ref[...] = _dot(xb, wg_ref[...])
    sm = _dot(xb, wsm_ref[...])
    lane = lax.broadcasted_iota(jnp.int32, sm.shape, 1)
    sm_ref[...] = jnp.where(lane < SM_USED_DT, _softplus(sm + dtbias_ref[...]), sm)


def _proj_call(x, wz, wxbc, wq, wk, wv, wg, wsm, convw, convb, dtbias):
    B, L, _ = x.shape
    tm = PROJ_BLOCK
    nb = L // tm
    hb = tm // HALO
    const = lambda shape: pl.BlockSpec(shape, lambda b, i: (0,) * len(shape), pipeline_mode=pl.Buffered(1))
    tok = lambda w: pl.BlockSpec((None, tm, w), lambda b, i: (b, i, 0))
    outs = ((SSD_WIDTH, F32), (SSD_WIDTH, F32), (BC_WIDTH, BF16), (GLA_KEY, F32), (GLA_KEY, F32), (GLA_VAL, BF16),
            (GLA_VAL, F32), (LANES, F32))
    return pl.pallas_call(
        _proj_kernel,
        grid=(B, nb),
        in_specs=[
            tok(D_MODEL),
            pl.BlockSpec((None, HALO, D_MODEL), lambda b, i: (b, jnp.maximum(i * hb - 1, 0), 0)),
            pl.BlockSpec((None, HALO, D_MODEL), lambda b, i: (b, jnp.minimum((i + 1) * hb, L // HALO - 1), 0)),
            const(wz.shape), const(wxbc.shape), const(wq.shape), const(wk.shape), const(wv.shape),
            const(wg.shape), const(wsm.shape), const(convw.shape), const(convb.shape), const(dtbias.shape),
        ],
        out_specs=[tok(w) for w, _ in outs],
        out_shape=[jax.ShapeDtypeStruct((B, L, w), dt) for w, dt in outs],
        scratch_shapes=[pltpu.VMEM((CONV_CH // LANES, tm + 2 * HALO, LANES), F32)],
        compiler_params=pltpu.CompilerParams(dimension_semantics=("parallel", "parallel"),
                                             vmem_limit_bytes=VMEM_LIMIT_BYTES),
        name="proj",
    )(x, x, x, wz, wxbc, wq, wk, wv, wg, wsm, convw, convb, dtbias)


def _ssd_prep(reverse, sm_ref, xs_ref, alog_ref, tri_ref, e2_ref):
    q = SSD_CHUNK
    tb = sm_ref.shape[0]
    sm = sm_ref[...]
    lane1 = lax.broadcasted_iota(jnp.int32, (1, LANES), 1)
    a_neg = jnp.where(lane1 < SM_USED_DT, -jnp.exp(alog_ref[...]) * LOG2E, 0.0)
    hi, mid, lo = _split3(sm * a_neg)
    tri = tri_ref[...]
    acs_c, last_c = [], []
    for r0 in range(0, tb, q):
        rows = slice(r0, r0 + q)
        a = _dot(tri, jnp.concatenate([hi[rows], mid[rows], lo[rows]], axis=0))
        acs_c.append(a)
        last_c.append(jnp.broadcast_to(a[0:1, :] if reverse else a[q - 1:q, :], (q, LANES)))
    acs = jnp.concatenate(acs_c, axis=0)
    dte = jnp.exp2(jnp.concatenate(last_c, axis=0) - acs)
    e2 = e2_ref[...]
    dt_x = _expand_heads(sm, e2)
    dtd_x = _expand_heads(sm * dte, e2)
    eacs_x = _expand_heads(jnp.exp2(acs), e2)
    xs = xs_ref[...]
    xdt = (xs * dt_x).astype(BF16)
    lane_w = lax.broadcasted_iota(jnp.int32, (tb, SSD_WIDTH), 1)
    even = (lane_w & SSD_HEADDIM) == 0
    zero = jnp.zeros_like(xdt)
    xdt_e = jnp.where(even, xdt, zero)
    xdt_o = jnp.where(even, zero, xdt)
    xdd = (xs * dtd_x).astype(BF16)
    preps = []
    for c, r0 in enumerate(range(0, tb, q)):
        rows = slice(r0, r0 + q)
        ex = eacs_x[rows]
        cdec_x = ex[0:1, :] if reverse else ex[q - 1:q, :]
        preps.append(dict(acs=acs_c[c], acs_t=acs_c[c].T, xdt_e=xdt_e[rows], xdt_o=xdt_o[rows], xdd=xdd[rows],
                          eacs_x=ex, cdec_x=cdec_x))
    return preps


def _ssd_intra(reverse, r0, p, bc_ref):
    q = SSD_CHUNK
    rows = slice(r0, r0 + q)
    lane_off = SM_DTB if reverse else SM_DTF
    li = lax.broadcasted_iota(jnp.int32, (q, q), 0)
    si = lax.broadcasted_iota(jnp.int32, (q, q), 1)
    mask = (si >= li) if reverse else (li >= si)
    acs, acs_t = p["acs"], p["acs_t"]
    ys, cgs, snews = [], [], []
    for g in range(SSD_GROUPS):
        b0 = g * SSD_STATE
        c0 = SSD_GROUPS * SSD_STATE + g * SSD_STATE
        bg = bc_ref[rows, b0:b0 + SSD_STATE]
        cg = bc_ref[rows, c0:c0 + SSD_STATE]
        cb = _dot_nt(cg, bg)
        for pr in range(SSD_GW // LANES):
            pl0 = g * SSD_GW + pr * LANES
            ms = []
            for hh in range(LANES // SSD_HEADDIM):
                col = lane_off + g * SSD_HPG + pr * (LANES // SSD_HEADDIM) + hh
                seg = acs[:, col:col + 1] - acs_t[col:col + 1, :]
                lm = jnp.exp2(jnp.where(mask, seg, -jnp.inf))
                ms.append((cb * lm).astype(BF16))
            rhs = jnp.concatenate([p["xdt_e"][:, pl0:pl0 + LANES], p["xdt_o"][:, pl0:pl0 + LANES]], axis=0)
            ys.append(_dot(jnp.concatenate(ms, axis=1), rhs))
        gl = slice(g * SSD_GW, (g + 1) * SSD_GW)
        cgs.append(cg)
        snews.append(_dot_tn(bg, p["xdd"][:, gl]))
    return ys, cgs, snews


def _ssd_carry(p, ys, cgs, snews, states):
    outs, new_states = [], []
    for g in range(SSD_GROUPS):
        gl = slice(g * SSD_GW, (g + 1) * SSD_GW)
        s_in = states[g]
        y_off = _dot(cgs[g], s_in.astype(BF16)) * p["eacs_x"][:, gl]
        npr = SSD_GW // LANES
        for pr in range(npr):
            outs.append(ys[g * npr + pr] + y_off[:, pr * LANES:(pr + 1) * LANES])
        new_states.append(s_in * p["cdec_x"][:, gl] + snews[g])
    return jnp.concatenate(outs, axis=1), new_states


def _gla_prep(reverse, sm_ref, q_ref, k_ref, wgk_ref, bgk_ref, tri_ref):
    tb = sm_ref.shape[0]
    q = GLA_CHUNK
    u = _dot(sm_ref[...].astype(BF16), wgk_ref[...]) + bgk_ref[...]
    lg = (jnp.minimum(u, 0.0) * LOG2E - jnp.log2(1.0 + jnp.exp2(-jnp.abs(u) * LOG2E))) / GLA_GATE_NORM
    hi, mid, lo = _split3(lg)
    tri = tri_ref[...]
    bcs_c, last_c = [], []
    for r0 in range(0, tb, q):
        rows = slice(r0, r0 + q)
        b = _dot(tri, jnp.concatenate([hi[rows], mid[rows], lo[rows]], axis=0))
        bcs_c.append(b)
        last_c.append(b[0:1, :] if reverse else b[q - 1:q, :])
    bcs = jnp.concatenate(bcs_c, axis=0)
    last_b = jnp.concatenate([jnp.broadcast_to(l, (q, GLA_KEY)) for l in last_c], axis=0)
    kk = k_ref[...]
    qt = (q_ref[...] * jnp.exp2(bcs)).astype(BF16)
    kt = (kk * jnp.exp2(-bcs)).astype(BF16)
    ke = (kk * jnp.exp2(last_b - bcs)).astype(BF16)
    dec_cols = [[jnp.broadcast_to(jnp.exp2(l[:, h * GLA_DK:(h + 1) * GLA_DK]), (GLA_DK, GLA_DK)).T
                 for h in range(GLA_HEADS)] for l in last_c]
    return qt, kt, ke, dec_cols


def _gla_intra(reverse, r0, qt, kt, ke, v_ref):
    q = GLA_CHUNK
    rows = slice(r0, r0 + q)
    li = lax.broadcasted_iota(jnp.int32, (q, q), 0)
    si = lax.broadcasted_iota(jnp.int32, (q, q), 1)
    mask = (si >= li) if reverse else (li >= si)
    atts, ups = [], []
    for h in range(GLA_HEADS):
        kl = slice(h * GLA_DK, (h + 1) * GLA_DK)
        vl = slice(h * GLA_DV, (h + 1) * GLA_DV)
        atts.append(jnp.where(mask, _dot_nt(qt[rows, kl], kt[rows, kl]), 0.0).astype(BF16))
        ups.append(_dot_tn(ke[rows, kl], v_ref[rows, vl]))
    return atts, ups


def _gla_carry(r0, qt, dec_cols, atts, ups, v_ref, states):
    rows = slice(r0, r0 + GLA_CHUNK)
    outs, new_states = [], []
    for h in range(GLA_HEADS):
        kl = slice(h * GLA_DK, (h + 1) * GLA_DK)
        vl = slice(h * GLA_DV, (h + 1) * GLA_DV)
        st = states[h]
        lhs = jnp.concatenate([qt[rows, kl], atts[h]], axis=1)
        rhs = jnp.concatenate([st.astype(BF16), v_ref[rows, vl]], axis=0)
        outs.append(_dot(lhs, rhs))
        dcol = dec_cols[h]
        new_states.append(st * jnp.concatenate([dcol, dcol], axis=1) + ups[h])
    return jnp.concatenate(outs, axis=1), new_states


def _run_scans(reverse, sm_ref, xs_ref, bc_ref, q_ref, k_ref, v_ref, alog_ref, wgk_ref, bgk_ref,
               tri_s_ref, tri_g_ref, e2_ref, ssd_state, gla_state):
    tb = sm_ref.shape[0]

    @pl.when(pl.program_id(1) == 0)
    def _():
        ssd_state[...] = jnp.zeros_like(ssd_state)
        gla_state[...] = jnp.zeros_like(gla_state)

    s_states = [ssd_state[g] for g in range(SSD_GROUPS)]
    g_states = [gla_state[h] for h in range(GLA_HEADS)]
    s_chunks = list(range(0, tb, SSD_CHUNK))
    g_chunks = list(range(0, tb, GLA_CHUNK))
    preps = _ssd_prep(reverse, sm_ref, xs_ref, alog_ref, tri_s_ref, e2_ref)
    qt, kt, ke, dec_cols = _gla_prep(reverse, sm_ref, q_ref, k_ref, wgk_ref, bgk_ref, tri_g_ref)
    s_intra = [_ssd_intra(reverse, r0, p, bc_ref) for r0, p in zip(s_chunks, preps)]
    g_intra = [_gla_intra(reverse, r0, qt, kt, ke, v_ref) for r0 in g_chunks]
    s_order = range(len(s_chunks) - 1, -1, -1) if reverse else range(len(s_chunks))
    g_order = range(len(g_chunks) - 1, -1, -1) if reverse else range(len(g_chunks))
    y_c = [None] * len(s_chunks)
    o_c = [None] * len(g_chunks)
    for c in s_order:
        y_c[c], s_states = _ssd_carry(preps[c], *s_intra[c], s_states)
    for c in g_order:
        o_c[c], g_states = _gla_carry(g_chunks[c], qt, dec_cols[c], *g_intra[c], v_ref, g_states)
    return jnp.concatenate(y_c, axis=0), jnp.concatenate(o_c, axis=0), s_states, g_states


def _store_states(ssd_state, gla_state, s_states, g_states):
    for g in range(SSD_GROUPS):
        ssd_state[g] = s_states[g]
    for h in range(GLA_HEADS):
        gla_state[h] = g_states[h]


def _scan_bwd_kernel(sm_ref, xs_ref, bc_ref, q_ref, k_ref, v_ref, alog_ref, wgk_ref, bgk_ref,
                     tri_s_ref, tri_g_ref, e2_ref, y_ref, o_ref, ssd_state, gla_state):
    y, o, s_states, g_states = _run_scans(True, sm_ref, xs_ref, bc_ref, q_ref, k_ref, v_ref, alog_ref, wgk_ref,
                                          bgk_ref, tri_s_ref, tri_g_ref, e2_ref, ssd_state, gla_state)
    y_ref[...] = y
    o_ref[...] = o
    _store_states(ssd_state, gla_state, s_states, g_states)


def _scan_fwd_kernel(sm_ref, xs_ref, bc_ref, q_ref, k_ref, v_ref, alog_ref, wgk_ref, bgk_ref,
                     tri_s_ref, tri_g_ref, e2_ref,
                     x_ref, z_ref, g_ref, yb_ref, ob_ref, dskip_ref, ssdg_ref, glag_ref, wout_ref,
                     ln_g_ref, ln_b_ref,
                     x1_ref, ssd_state, gla_state):
    yf, of, s_states, g_states = _run_scans(False, sm_ref, xs_ref, bc_ref, q_ref, k_ref, v_ref, alog_ref, wgk_ref,
                                            bgk_ref, tri_s_ref, tri_g_ref, e2_ref, ssd_state, gla_state)
    acc = ALPHA * x_ref[...]
    gw = SSD_WIDTH // SSD_GROUPS
    for g in range(SSD_GROUPS):
        sl = slice(g * gw, (g + 1) * gw)
        y = yf[:, sl] + yb_ref[:, sl] + dskip_ref[:, sl] * xs_ref[:, sl]
        y = y * _silu(z_ref[:, sl])
        ms = jnp.mean(y * y, axis=-1, keepdims=True)
        yn = y * lax.rsqrt(ms + RMS_EPS) * ssdg_ref[:, sl]
        acc = acc + _dot(yn.astype(BF16), wout_ref[sl, :])
    for h in range(GLA_HEADS):
        sl = slice(h * GLA_DV, (h + 1) * GLA_DV)
        o = of[:, sl] + ob_ref[:, sl]
        ms = jnp.mean(o * o, axis=-1, keepdims=True)
        on = o * lax.rsqrt(ms + RMS_EPS) * glag_ref[...] * _silu(g_ref[:, sl])
        acc = acc + _dot(on.astype(BF16), wout_ref[SSD_WIDTH + h * GLA_DV:SSD_WIDTH + (h + 1) * GLA_DV, :])
    x1_ref[...] = _layer_norm(acc, ln_g_ref[...], ln_b_ref[...])
    _store_states(ssd_state, gla_state, s_states, g_states)


def _scan_common_specs(L, tb, reverse, consts):
    nb = L // tb
    if reverse:
        tok = lambda w: pl.BlockSpec((None, tb, w), lambda b, i: (b, nb - 1 - i, 0))
    else:
        tok = lambda w: pl.BlockSpec((None, tb, w), lambda b, i: (b, i, 0))
    const = lambda a: pl.BlockSpec(a.shape, lambda b, i: (0,) * a.ndim, pipeline_mode=pl.Buffered(1))
    return tok, [const(a) for a in consts]


def _scan_bwd_call(sm, xs, bc, q, k, v, alog, wgk, bgk, tri_s, tri_g, e2):
    B, L, _ = sm.shape
    tb = SCAN_BLOCK
    consts = (alog, wgk, bgk, tri_s, tri_g, e2)
    tok, const_specs = _scan_common_specs(L, tb, True, consts)
    return pl.pallas_call(
        _scan_bwd_kernel,
        grid=(B, L // tb),
        in_specs=[tok(LANES), tok(SSD_WIDTH), tok(BC_WIDTH), tok(GLA_KEY), tok(GLA_KEY), tok(GLA_VAL)] + const_specs,
        out_specs=[tok(SSD_WIDTH), tok(GLA_VAL)],
        out_shape=[jax.ShapeDtypeStruct((B, L, SSD_WIDTH), F32), jax.ShapeDtypeStruct((B, L, GLA_VAL), F32)],
        scratch_shapes=[pltpu.VMEM((SSD_GROUPS, SSD_STATE, SSD_GW), F32),
                        pltpu.VMEM((GLA_HEADS, GLA_DK, GLA_DV), F32)],
        compiler_params=pltpu.CompilerParams(dimension_semantics=("arbitrary", "arbitrary"),
                                             vmem_limit_bytes=VMEM_LIMIT_BYTES),
        name="scan_bwd",
    )(sm, xs, bc, q, k, v, *consts)


def _scan_fwd_call(sm, xs, bc, q, k, v, alog, wgk, bgk, tri_s, tri_g, e2,
                   x, z, g, yb, ob, dskip, ssdg, glag, wout, ln_g, ln_b):
    B, L, _ = sm.shape
    tb = SCAN_BLOCK
    consts_a = (alog, wgk, bgk, tri_s, tri_g, e2)
    consts_b = (dskip, ssdg, glag, wout, ln_g, ln_b)
    tok, const_a = _scan_common_specs(L, tb, False, consts_a)
    _, const_b = _scan_common_specs(L, tb, False, consts_b)
    return pl.pallas_call(
        _scan_fwd_kernel,
        grid=(B, L // tb),
        in_specs=([tok(LANES), tok(SSD_WIDTH), tok(BC_WIDTH), tok(GLA_KEY), tok(GLA_KEY), tok(GLA_VAL)] + const_a
                  + [tok(D_MODEL), tok(SSD_WIDTH), tok(GLA_VAL), tok(SSD_WIDTH), tok(GLA_VAL)] + const_b),
        out_specs=tok(D_MODEL),
        out_shape=jax.ShapeDtypeStruct((B, L, D_MODEL), F32),
        scratch_shapes=[pltpu.VMEM((SSD_GROUPS, SSD_STATE, SSD_GW), F32),
                        pltpu.VMEM((GLA_HEADS, GLA_DK, GLA_DV), F32)],
        compiler_params=pltpu.CompilerParams(dimension_semantics=("arbitrary", "arbitrary"),
                                             vmem_limit_bytes=VMEM_LIMIT_BYTES),
        name="scan_fwd",
    )(sm, xs, bc, q, k, v, *consts_a, x, z, g, yb, ob, *consts_b)


def _memkv_kernel(mem_ref, wk_ref, wv_ref, k_ref, v_ref):
    m = mem_ref[...].astype(BF16)
    k_ref[...] = _dot(m, wk_ref[...]).astype(BF16)
    v_ref[...] = _dot(m, wv_ref[...]).astype(BF16)


def _memkv_call(mem, wk, wv):
    B, M, _ = mem.shape
    blk = pl.BlockSpec((None, M, D_MODEL), lambda b: (b, 0, 0))
    const = lambda a: pl.BlockSpec(a.shape, lambda b: (0,) * a.ndim, pipeline_mode=pl.Buffered(1))
    return pl.pallas_call(
        _memkv_kernel,
        grid=(B,),
        in_specs=[blk, const(wk), const(wv)],
        out_specs=[blk, blk],
        out_shape=[jax.ShapeDtypeStruct((B, M, D_MODEL), BF16)] * 2,
        compiler_params=pltpu.CompilerParams(dimension_semantics=("parallel",),
                                             vmem_limit_bytes=VMEM_LIMIT_BYTES),
        name="memkv",
    )(mem, wk, wv)


def _attn_part(subs, x_ref, mk_ref, mv_ref, wq_ref, wo_ref, ln2g_ref, ln2b_ref):
    x1s = [x_ref[rows, :] for rows in subs]
    qfs = [_dot(x1.astype(BF16), wq_ref[...]) for x1 in x1s]
    accs = [ALPHA * x1 for x1 in x1s]
    for h in range(MEM_HEADS):
        sl = slice(h * MEM_HD, (h + 1) * MEM_HD)
        ss = [_dot_nt(qf[:, sl].astype(BF16), mk_ref[:, sl]) * (MEM_HD ** -0.5) for qf in qfs]
        ps = []
        for s in ss:
            e = jnp.exp(s - jnp.max(s, axis=-1, keepdims=True))
            ps.append((e / jnp.sum(e, axis=-1, keepdims=True)).astype(BF16))
        ohs = [_dot(p, mv_ref[:, sl]).astype(BF16) for p in ps]
        accs = [acc + _dot(oh, wo_ref[sl, :]) for acc, oh in zip(accs, ohs)]
    return [_layer_norm(acc, ln2g_ref[...], ln2b_ref[...]) for acc in accs]


def _mlp_part(subs, x2s, w1_ref, w2_ref, ln3g_ref, ln3b_ref, out_ref):
    x2bs = [x2.astype(BF16) for x2 in x2s]
    accs = [ALPHA * x2 for x2 in x2s]
    for c in range(0, D_FF, FF_CHUNK):
        hs = [jnp.maximum(_dot(x2b, w1_ref[:, c:c + FF_CHUNK]), 0.0) for x2b in x2bs]
        accs = [acc + _dot((hdn * hdn).astype(BF16), w2_ref[c:c + FF_CHUNK, :]) for acc, hdn in zip(accs, hs)]
    for rows, acc in zip(subs, accs):
        out_ref[rows, :] = _layer_norm(acc, ln3g_ref[...], ln3b_ref[...])


def _attn_mlp_kernel(x_ref, mk_ref, mv_ref, wq_ref, wo_ref, ln2g_ref, ln2b_ref, w1_ref, w2_ref,
                     ln3g_ref, ln3b_ref, out_ref):
    subs = [slice(r, r + ATTN_SUB) for r in range(0, x_ref.shape[0], ATTN_SUB)]
    x2s = _attn_part(subs, x_ref, mk_ref, mv_ref, wq_ref, wo_ref, ln2g_ref, ln2b_ref)
    _mlp_part(subs, x2s, w1_ref, w2_ref, ln3g_ref, ln3b_ref, out_ref)


def _attn_mlp_call(x1, mk, mv, wq, wo, ln2g, ln2b, w1, w2, ln3g, ln3b):
    B, L, _ = x1.shape
    M = mk.shape[1]
    tm = ATTN_BLOCK
    tok = pl.BlockSpec((None, tm, D_MODEL), lambda b, i: (b, i, 0))
    memb = pl.BlockSpec((None, M, D_MODEL), lambda b, i: (b, 0, 0))
    const = lambda a: pl.BlockSpec(a.shape, lambda b, i: (0,) * a.ndim, pipeline_mode=pl.Buffered(1))
    return pl.pallas_call(
        _attn_mlp_kernel,
        grid=(B, L // tm),
        in_specs=[tok, memb, memb, const(wq), const(wo), const(ln2g), const(ln2b), const(w1), const(w2),
                  const(ln3g), const(ln3b)],
        out_specs=tok,
        out_shape=jax.ShapeDtypeStruct((B, L, D_MODEL), F32),
        compiler_params=pltpu.CompilerParams(dimension_semantics=("parallel", "parallel"),
                                             vmem_limit_bytes=VMEM_LIMIT_BYTES),
        name="attn_mlp",
    )(x1, mk, mv, wq, wo, ln2g, ln2b, w1, w2, ln3g, ln3b)


def _prepare(w_in, conv_w, conv_b, a_log_f, a_log_b, dt_bias_f, dt_bias_b, d_skip, ssd_norm_g,
             w_gk_f, b_gk_f, w_gk_b, b_gk_b, gla_norm_g, w_out, ln1_g, ln1_b,
             w_mq, w_mk, w_mv, w_mo, ln2_g, ln2_b, w_ff1, w_ff2, ln3_g, ln3_b):
    offs = [0]
    for s in IN_SIZES:
        offs.append(offs[-1] + s)
    col = lambda i: w_in[:, offs[i]:offs[i + 1]]
    wz = col(0).astype(BF16)
    wxbc = jnp.concatenate([col(1), col(2), col(3)], axis=1).astype(BF16)
    wq, wk, wv, wg = (col(i).astype(BF16) for i in (6, 7, 8, 9))
    pad = LANES - 2 * SSD_HEADS - GLA_LOWRANK
    wsm = jnp.concatenate([col(4), col(5), col(10), jnp.zeros((D_MODEL, pad), F32)], axis=1).astype(BF16)
    row = lambda a: a.reshape(1, -1).astype(F32)
    zpad = jnp.zeros((1, LANES - 2 * SSD_HEADS), F32)
    dtbias = jnp.concatenate([row(dt_bias_f), row(dt_bias_b), zpad], axis=1)
    alog = jnp.concatenate([row(a_log_f), row(a_log_b), zpad], axis=1)

    def gk_pad(w):
        return jnp.zeros((LANES, GLA_KEY), F32).at[SM_LR:SM_LR + GLA_LOWRANK].set(w).astype(BF16)

    def tri3(qn, reverse):
        r = jnp.arange(qn)[:, None]
        c = jnp.arange(qn)[None, :]
        t = (c >= r) if reverse else (r >= c)
        return jnp.tile(t.astype(BF16), (1, 3))

    def e2(lane_off):
        r = jnp.arange(LANES)[:, None]
        c = jnp.arange(SSD_WIDTH)[None, :]
        e = (c // SSD_HEADDIM == r - lane_off).astype(BF16)
        return jnp.concatenate([e, e], axis=0)

    return dict(
        proj=(wz, wxbc, wq, wk, wv, wg, wsm, conv_w.astype(F32), row(conv_b), dtbias),
        bwd=(alog, gk_pad(w_gk_b), row(b_gk_b), tri3(SSD_CHUNK, True), tri3(GLA_CHUNK, True), e2(SM_DTB)),
        fwd=(alog, gk_pad(w_gk_f), row(b_gk_f), tri3(SSD_CHUNK, False), tri3(GLA_CHUNK, False), e2(SM_DTF)),
        comb=(row(jnp.repeat(d_skip, SSD_HEADDIM)), row(ssd_norm_g), row(gla_norm_g), w_out.astype(BF16),
              row(ln1_g), row(ln1_b)),
        memkv=(w_mk.astype(BF16), w_mv.astype(BF16)),
        attn=(w_mq.astype(BF16), w_mo.astype(BF16), row(ln2_g), row(ln2_b), w_ff1.astype(BF16),
              w_ff2.astype(BF16), row(ln3_g), row(ln3_b)),
    )


def _encoder_layer(x, mem, p):
    z, xs, bc, q, k, v, g, sm = _proj_call(x, *p["proj"])
    yb, ob = _scan_bwd_call(sm, xs, bc, q, k, v, *p["bwd"])
    x1 = _scan_fwd_call(sm, xs, bc, q, k, v, *p["fwd"], x, z, g, yb, ob, *p["comb"])
    mk, mv = _memkv_call(mem, *p["memkv"])
    return _attn_mlp_call(x1, mk, mv, *p["attn"])


def kernel(x_prompt, x_sample, mem_prompt, mem_sample, w_in, conv_w, conv_b, a_log_f, a_log_b, dt_bias_f,
           dt_bias_b, d_skip, ssd_norm_g, w_gk_f, b_gk_f, w_gk_b, b_gk_b, gla_norm_g, w_out, ln1_g, ln1_b,
           w_mq, w_mk, w_mv, w_mo, ln2_g, ln2_b, w_ff1, w_ff2, ln3_g, ln3_b):
    params = (w_in, conv_w, conv_b, a_log_f, a_log_b, dt_bias_f, dt_bias_b, d_skip, ssd_norm_g,
              w_gk_f, b_gk_f, w_gk_b, b_gk_b, gla_norm_g, w_out, ln1_g, ln1_b,
              w_mq, w_mk, w_mv, w_mo, ln2_g, ln2_b, w_ff1, w_ff2, ln3_g, ln3_b)
    assert all(a.shape[0] == DEPTH for a in params)
    p = _prepare(*[a[0] for a in params])
    return (_encoder_layer(x_prompt, mem_prompt, p), _encoder_layer(x_sample, mem_sample, p))
```

```python
import jax
import jax.numpy as jnp
from jax import lax
from jax.experimental import pallas as pl
from jax.experimental.pallas import tpu as pltpu

F32 = jnp.float32
BF16 = jnp.bfloat16

D_MODEL = 1024
SSD_WIDTH = 1024
SSD_HEADDIM = 64
SSD_HEADS = 16
SSD_GROUPS = 2
SSD_HPG = 8
SSD_STATE = 128
SSD_CHUNK = 128
SSD_GW = SSD_HPG * SSD_HEADDIM
CONV_K = 5
CONV_CH = SSD_WIDTH + 2 * SSD_GROUPS * SSD_STATE
BC_WIDTH = 2 * SSD_GROUPS * SSD_STATE
GLA_HEADS = 4
GLA_KEY = 512
GLA_VAL = 1024
GLA_DK = 128
GLA_DV = 256
GLA_LOWRANK = 16
GLA_GATE_NORM = 16.0
GLA_CHUNK = 64
MIX_WIDTH = SSD_WIDTH + GLA_VAL
IN_SIZES = (SSD_WIDTH, SSD_WIDTH, SSD_GROUPS * SSD_STATE, SSD_GROUPS * SSD_STATE, SSD_HEADS, SSD_HEADS,
            GLA_KEY, GLA_KEY, GLA_VAL, GLA_VAL, GLA_LOWRANK)
MEM_HEADS = 4
MEM_HD = 256
D_FF = 4096
DEPTH = 1
ALPHA = (2 * DEPTH) ** 0.25
LN_EPS = 1e-5
RMS_EPS = 1e-5
LOG2E = 1.4426950408889634

LANES = 128
SUBLANES = 8
VMEM_LIMIT_BYTES = 56 * 1024 * 1024

SM_DTF = 0
SM_DTB = SSD_HEADS
SM_LR = 2 * SSD_HEADS
SM_USED_DT = 2 * SSD_HEADS

HALO = SUBLANES
CONV_L = (CONV_K - 1) // 2

PROJ_BLOCK = 512
CONV_COLS = 512
SCAN_BLOCK = 512
ATTN_BLOCK = 1024
ATTN_SUB = 256
FF_CHUNK = 1024


def _dot(a, b):
    return jnp.dot(a, b, preferred_element_type=F32)


def _dot_nt(a, b):
    return lax.dot_general(a, b, (((1,), (1,)), ((), ())), preferred_element_type=F32)


def _dot_tn(a, b):
    return lax.dot_general(a, b, (((0,), (0,)), ((), ())), preferred_element_type=F32)


def _sigmoid(x):
    return 1.0 / (1.0 + jnp.exp(-x))


def _silu(x):
    return x * _sigmoid(x)


def _softplus(x):
    return jnp.maximum(x, 0.0) + jnp.log1p(jnp.exp(-jnp.abs(x)))


def _split3(a):
    hi = a.astype(BF16)
    r = a - hi.astype(F32)
    mid = r.astype(BF16)
    lo = (r - mid.astype(F32)).astype(BF16)
    return hi, mid, lo


def _cumsum_rows(tri3, a):
    hi, mid, lo = _split3(a)
    return _dot(tri3, jnp.concatenate([hi, mid, lo], axis=0))


def _expand_heads(a, e2):
    hi = a.astype(BF16)
    lo = (a - hi.astype(F32)).astype(BF16)
    return _dot(jnp.concatenate([hi, lo], axis=1), e2)


def _layer_norm(h, g, b):
    mu = jnp.mean(h, axis=-1, keepdims=True)
    d = h - mu
    var = jnp.mean(d * d, axis=-1, keepdims=True)
    return d * lax.rsqrt(var + LN_EPS) * g + b


def _proj_kernel(x_ref, xp_ref, xn_ref, wz_ref, wxbc_ref, wq_ref, wk_ref, wv_ref, wg_ref, wsm_ref,
                 convw_ref, convb_ref, dtbias_ref,
                 z_ref, xs_ref, bc_ref, q_ref, k_ref, v_ref, g_ref, sm_ref, ext_ref):
    i = pl.program_id(1)
    nb = pl.num_programs(1)
    tm = x_ref.shape[0]
    xb = x_ref[...].astype(BF16)
    has_prev = i > 0
    has_next = i < nb - 1
    x_ext = jnp.concatenate([xp_ref[...], x_ref[...], xn_ref[...]], axis=0).astype(BF16)
    row = lax.broadcasted_iota(jnp.int32, (tm + 2 * HALO, 1), 0)
    keep = jnp.logical_and(jnp.logical_or(row >= HALO, has_prev), jnp.logical_or(row < HALO + tm, has_next))
    cw = CONV_COLS
    for c in range(0, CONV_CH, cw):
        u = jnp.where(keep, _dot(x_ext, wxbc_ref[:, c:c + cw]), 0.0)
        for t in range(cw // LANES):
            ext_ref[c // LANES + t, :, :] = u[:, t * LANES:(t + 1) * LANES]

    def conv_tile(ct):
        ls = slice(ct * LANES, (ct + 1) * LANES)
        acc = jnp.broadcast_to(convb_ref[:, ls], (tm, LANES))
        for j in range(CONV_K):
            r0 = HALO - CONV_L + j
            acc = acc + convw_ref[j:j + 1, ls] * ext_ref[ct, r0:r0 + tm, :]
        return _silu(acc)

    n_xs = SSD_WIDTH // LANES
    for ct in range(n_xs):
        xs_ref[:, ct * LANES:(ct + 1) * LANES] = conv_tile(ct)
    for ct in range(n_xs, CONV_CH // LANES):
        bc_ref[:, (ct - n_xs) * LANES:(ct - n_xs + 1) * LANES] = conv_tile(ct).astype(BF16)

    z_ref[...] = _dot(xb, wz_ref[...])
    q_ref[...] = _dot(xb, wq_ref[...]) * (GLA_DK ** -0.5)
    k_ref[...] = _dot(xb, wk_ref[...])
    v_ref[...] = _dot(xb, wv_ref[...]).astype(BF16)
    g_ref[...] = _dot(xb, wg_ref[...])
    sm = _dot(xb, wsm_ref[...])
    lane = lax.broadcasted_iota(jnp.int32, sm.shape, 1)
    sm_ref[...] = jnp.where(lane < SM_USED_DT, _softplus(sm + dtbias_ref[...]), sm)


def _proj_call(x, wz, wxbc, wq, wk, wv, wg, wsm, convw, convb, dtbias):
    B, L, _ = x.shape
    tm = PROJ_BLOCK
    nb = L // tm
    hb = tm // HALO
    const = lambda shape: pl.BlockSpec(shape, lambda b, i: (0,) * len(shape), pipeline_mode=pl.Buffered(1))
    tok = lambda w: pl.BlockSpec((None, tm, w), lambda b, i: (b, i, 0))
    outs = ((SSD_WIDTH, F32), (SSD_WIDTH, F32), (BC_WIDTH, BF16), (GLA_KEY, F32), (GLA_KEY, F32), (GLA_VAL, BF16),
            (GLA_VAL, F32), (LANES, F32))
    return pl.pallas_call(
        _proj_kernel,
        grid=(B, nb),
        in_specs=[
            tok(D_MODEL),
            pl.BlockSpec((None, HALO, D_MODEL), lambda b, i: (b, jnp.maximum(i * hb - 1, 0), 0)),
            pl.BlockSpec((None, HALO, D_MODEL), lambda b, i: (b, jnp.minimum((i + 1) * hb, L // HALO - 1), 0)),
            const(wz.shape), const(wxbc.shape), const(wq.shape), const(wk.shape), const(wv.shape),
            const(wg.shape), const(wsm.shape), const(convw.shape), const(convb.shape), const(dtbias.shape),
        ],
        out_specs=[tok(w) for w, _ in outs],
        out_shape=[jax.ShapeDtypeStruct((B, L, w), dt) for w, dt in outs],
        scratch_shapes=[pltpu.VMEM((CONV_CH // LANES, tm + 2 * HALO, LANES), F32)],
        compiler_params=pltpu.CompilerParams(dimension_semantics=("parallel", "parallel"),
                                             vmem_limit_bytes=VMEM_LIMIT_BYTES),
        name="proj",
    )(x, x, x, wz, wxbc, wq, wk, wv, wg, wsm, convw, convb, dtbias)


def _ssd_prep(reverse, sm_ref, xs_ref, alog_ref, tri_ref, e2_ref):
    q = SSD_CHUNK
    tb = sm_ref.shape[0]
    sm = sm_ref[...]
    lane1 = lax.broadcasted_iota(jnp.int32, (1, LANES), 1)
    a_neg = jnp.where(lane1 < SM_USED_DT, -jnp.exp(alog_ref[...]) * LOG2E, 0.0)
    hi, mid, lo = _split3(sm * a_neg)
    tri = tri_ref[...]
    acs_c, last_c = [], []
    for r0 in range(0, tb, q):
        rows = slice(r0, r0 + q)
        a = _dot(tri, jnp.concatenate([hi[rows], mid[rows], lo[rows]], axis=0))
        acs_c.append(a)
        last_c.append(jnp.broadcast_to(a[0:1, :] if reverse else a[q - 1:q, :], (q, LANES)))
    acs = jnp.concatenate(acs_c, axis=0)
    dte = jnp.exp2(jnp.concatenate(last_c, axis=0) - acs)
    e2 = e2_ref[...]
    dt_x = _expand_heads(sm, e2)
    dtd_x = _expand_heads(sm * dte, e2)
    eacs_x = _expand_heads(jnp.exp2(acs), e2)
    xs = xs_ref[...]
    xdt = (xs * dt_x).astype(BF16)
    lane_w = lax.broadcasted_iota(jnp.int32, (tb, SSD_WIDTH), 1)
    even = (lane_w & SSD_HEADDIM) == 0
    zero = jnp.zeros_like(xdt)
    xdt_e = jnp.where(even, xdt, zero)
    xdt_o = jnp.where(even, zero, xdt)
    xdd = (xs * dtd_x).astype(BF16)
    preps = []
    for c, r0 in enumerate(range(0, tb, q)):
        rows = slice(r0, r0 + q)
        ex = eacs_x[rows]
        cdec_x = ex[0:1, :] if reverse else ex[q - 1:q, :]
        preps.append(dict(acs=acs_c[c], acs_t=acs_c[c].T, xdt_e=xdt_e[rows], xdt_o=xdt_o[rows], xdd=xdd[rows],
                          eacs_x=ex, cdec_x=cdec_x))
    return preps


def _ssd_intra(reverse, r0, p, bc_ref):
    q = SSD_CHUNK
    rows = slice(r0, r0 + q)
    lane_off = SM_DTB if reverse else SM_DTF
    li = lax.broadcasted_iota(jnp.int32, (q, q), 0)
    si = lax.broadcasted_iota(jnp.int32, (q, q), 1)
    mask = (si >= li) if reverse else (li >= si)
    acs, acs_t = p["acs"], p["acs_t"]
    ys, cgs, snews = [], [], []
    for g in range(SSD_GROUPS):
        b0 = g * SSD_STATE
        c0 = SSD_GROUPS * SSD_STATE + g * SSD_STATE
        bg = bc_ref[rows, b0:b0 + SSD_STATE]
        cg = bc_ref[rows, c0:c0 + SSD_STATE]
        cb = _dot_nt(cg, bg)
        for pr in range(SSD_GW // LANES):
            pl0 = g * SSD_GW + pr * LANES
            ms = []
            for hh in range(LANES // SSD_HEADDIM):
                col = lane_off + g * SSD_HPG + pr * (LANES // SSD_HEADDIM) + hh
                seg = acs[:, col:col + 1] - acs_t[col:col + 1, :]
                lm = jnp.exp2(jnp.where(mask, seg, -jnp.inf))
                ms.append((cb * lm).astype(BF16))
            rhs = jnp.concatenate([p["xdt_e"][:, pl0:pl0 + LANES], p["xdt_o"][:, pl0:pl0 + LANES]], axis=0)
            ys.append(_dot(jnp.concatenate(ms, axis=1), rhs))
        gl = slice(g * SSD_GW, (g + 1) * SSD_GW)
        cgs.append(cg)
        snews.append(_dot_tn(bg, p["xdd"][:, gl]))
    return ys, cgs, snews


def _ssd_carry(p, ys, cgs, snews, states):
    outs, new_states = [], []
    for g in range(SSD_GROUPS):
        gl = slice(g * SSD_GW, (g + 1) * SSD_GW)
        s_in = states[g]
        y_off = _dot(cgs[g], s_in.astype(BF16)) * p["eacs_x"][:, gl]
        npr = SSD_GW // LANES
        for pr in range(npr):
            outs.append(ys[g * npr + pr] + y_off[:, pr * LANES:(pr + 1) * LANES])
        new_states.append(s_in * p["cdec_x"][:, gl] + snews[g])
    return jnp.concatenate(outs, axis=1), new_states


def _gla_prep(reverse, sm_ref, q_ref, k_ref, wgk_ref, bgk_ref, tri_ref):
    tb = sm_ref.shape[0]
    q = GLA_CHUNK
    u = _dot(sm_ref[...].astype(BF16), wgk_ref[...]) + bgk_ref[...]
    lg = (jnp.minimum(u, 0.0) * LOG2E - jnp.log2(1.0 + jnp.exp2(-jnp.abs(u) * LOG2E))) / GLA_GATE_NORM
    hi, mid, lo = _split3(lg)
    tri = tri_ref[...]
    bcs_c, last_c = [], []
    for r0 in range(0, tb, q):
        rows = slice(r0, r0 + q)
        b = _dot(tri, jnp.concatenate([hi[rows], mid[rows], lo[rows]], axis=0))
        bcs_c.append(b)
        last_c.append(b[0:1, :] if reverse else b[q - 1:q, :])
    nc = len(bcs_c)
    first = lambda c: (c % 2 == 1) if reverse else (c % 2 == 0)
    other = lambda c: c + 1 if c % 2 == 0 else c - 1
    zero_row = jnp.zeros((q, GLA_KEY), F32)
    bcs = jnp.concatenate(bcs_c, axis=0)
    last_b = jnp.concatenate([jnp.broadcast_to(l, (q, GLA_KEY)) for l in last_c], axis=0)
    q_shift = jnp.concatenate([zero_row if first(c) else jnp.broadcast_to(last_c[other(c)], (q, GLA_KEY))
                               for c in range(nc)], axis=0)
    k_shift = jnp.concatenate([jnp.broadcast_to(last_c[other(c)], (q, GLA_KEY)) if first(c) else zero_row
                               for c in range(nc)], axis=0)
    kk = k_ref[...]
    qq = q_ref[...]
    qt = (qq * jnp.exp2(bcs)).astype(BF16)
    qs = (qq * jnp.exp2(bcs + q_shift)).astype(BF16)
    kt = (kk * jnp.exp2(-bcs)).astype(BF16)
    ke = (kk * jnp.exp2(last_b - bcs)).astype(BF16)
    ks = (kk * jnp.exp2(last_b - bcs + k_shift)).astype(BF16)
    dec_cols = [[jnp.broadcast_to(jnp.exp2((last_c[2 * p] + last_c[2 * p + 1])[:, h * GLA_DK:(h + 1) * GLA_DK]),
                                  (GLA_DK, GLA_DK)).T for h in range(GLA_HEADS)] for p in range(nc // 2)]
    return qt, qs, kt, ke, ks, dec_cols


def _gla_intra(reverse, r0, qt, kt, ke, ks, v_ref):
    q = GLA_CHUNK
    rows = slice(r0, r0 + 2 * q)
    li = lax.broadcasted_iota(jnp.int32, (2 * q, 2 * q), 0)
    si = lax.broadcasted_iota(jnp.int32, (2 * q, 2 * q), 1)
    l_hi, s_hi = li >= q, si >= q
    diag = jnp.logical_and(l_hi == s_hi, (si >= li) if reverse else (li >= si))
    cross = jnp.logical_and(jnp.logical_not(l_hi), s_hi) if reverse else jnp.logical_and(l_hi, jnp.logical_not(s_hi))
    atts, ups = [], []
    for h in range(GLA_HEADS):
        kl = slice(h * GLA_DK, (h + 1) * GLA_DK)
        vl = slice(h * GLA_DV, (h + 1) * GLA_DV)
        p = _dot_nt(qt[rows, kl], jnp.concatenate([kt[rows, kl], ke[rows, kl]], axis=0))
        att = jnp.where(diag, p[:, :2 * q], jnp.where(cross, p[:, 2 * q:], 0.0))
        atts.append(att.astype(BF16))
        ups.append(_dot_tn(ks[rows, kl], v_ref[rows, vl]))
    return atts, ups


def _gla_carry(r0, qs, dec_cols, atts, ups, v_ref, states):
    rows = slice(r0, r0 + 2 * GLA_CHUNK)
    outs, new_states = [], []
    for h in range(GLA_HEADS):
        kl = slice(h * GLA_DK, (h + 1) * GLA_DK)
        vl = slice(h * GLA_DV, (h + 1) * GLA_DV)
        st = states[h]
        lhs = jnp.concatenate([qs[rows, kl], atts[h]], axis=1)
        rhs = jnp.concatenate([st.astype(BF16), v_ref[rows, vl]], axis=0)
        outs.append(_dot(lhs, rhs))
        dcol = dec_cols[h]
        new_states.append(st * jnp.concatenate([dcol, dcol], axis=1) + ups[h])
    return jnp.concatenate(outs, axis=1), new_states


def _run_scans(reverse, sm_ref, xs_ref, bc_ref, q_ref, k_ref, v_ref, alog_ref, wgk_ref, bgk_ref,
               tri_s_ref, tri_g_ref, e2_ref, ssd_state, gla_state):
    tb = sm_ref.shape[0]

    @pl.when(pl.program_id(1) == 0)
    def _():
        ssd_state[...] = jnp.zeros_like(ssd_state)
        gla_state[...] = jnp.zeros_like(gla_state)

    s_states = [ssd_state[g] for g in range(SSD_GROUPS)]
    g_states = [gla_state[h] for h in range(GLA_HEADS)]
    s_chunks = list(range(0, tb, SSD_CHUNK))
    g_chunks = list(range(0, tb, 2 * GLA_CHUNK))
    preps = _ssd_prep(reverse, sm_ref, xs_ref, alog_ref, tri_s_ref, e2_ref)
    qt, qs, kt, ke, ks, dec_cols = _gla_prep(reverse, sm_ref, q_ref, k_ref, wgk_ref, bgk_ref, tri_g_ref)
    s_intra = [_ssd_intra(reverse, r0, p, bc_ref) for r0, p in zip(s_chunks, preps)]
    g_intra = [_gla_intra(reverse, r0, qt, kt, ke, ks, v_ref) for r0 in g_chunks]
    s_order = range(len(s_chunks) - 1, -1, -1) if reverse else range(len(s_chunks))
    g_order = range(len(g_chunks) - 1, -1, -1) if reverse else range(len(g_chunks))
    y_c = [None] * len(s_chunks)
    o_c = [None] * len(g_chunks)
    for c in s_order:
        y_c[c], s_states = _ssd_carry(preps[c], *s_intra[c], s_states)
    for c in g_order:
        o_c[c], g_states = _gla_carry(g_chunks[c], qs, dec_cols[c], *g_intra[c], v_ref, g_states)
    return jnp.concatenate(y_c, axis=0), jnp.concatenate(o_c, axis=0), s_states, g_states


def _store_states(ssd_state, gla_state, s_states, g_states):
    for g in range(SSD_GROUPS):
        ssd_state[g] = s_states[g]
    for h in range(GLA_HEADS):
        gla_state[h] = g_states[h]


def _scan_bwd_kernel(sm_ref, xs_ref, bc_ref, q_ref, k_ref, v_ref, alog_ref, wgk_ref, bgk_ref,
                     tri_s_ref, tri_g_ref, e2_ref, y_ref, o_ref, ssd_state, gla_state):
    y, o, s_states, g_states = _run_scans(True, sm_ref, xs_ref, bc_ref, q_ref, k_ref, v_ref, alog_ref, wgk_ref,
                                          bgk_ref, tri_s_ref, tri_g_ref, e2_ref, ssd_state, gla_state)
    y_ref[...] = y
    o_ref[...] = o
    _store_states(ssd_state, gla_state, s_states, g_states)


def _scan_fwd_kernel(sm_ref, xs_ref, bc_ref, q_ref, k_ref, v_ref, alog_ref, wgk_ref, bgk_ref,
                     tri_s_ref, tri_g_ref, e2_ref,
                     x_ref, z_ref, g_ref, yb_ref, ob_ref, dskip_ref, ssdg_ref, glag_ref, wout_ref,
                     ln_g_ref, ln_b_ref,
                     x1_ref, ssd_state, gla_state):
    yf, of, s_states, g_states = _run_scans(False, sm_ref, xs_ref, bc_ref, q_ref, k_ref, v_ref, alog_ref, wgk_ref,
                                            bgk_ref, tri_s_ref, tri_g_ref, e2_ref, ssd_state, gla_state)
    acc = ALPHA * x_ref[...]
    gw = SSD_WIDTH // SSD_GROUPS
    for g in range(SSD_GROUPS):
        sl = slice(g * gw, (g + 1) * gw)
        y = yf[:, sl] + yb_ref[:, sl] + dskip_ref[:, sl] * xs_ref[:, sl]
        y = y * _silu(z_ref[:, sl])
        ms = jnp.mean(y * y, axis=-1, keepdims=True)
        yn = y * lax.rsqrt(ms + RMS_EPS) * ssdg_ref[:, sl]
        acc = acc + _dot(yn.astype(BF16), wout_ref[sl, :])
    for h in range(GLA_HEADS):
        sl = slice(h * GLA_DV, (h + 1) * GLA_DV)
        o = of[:, sl] + ob_ref[:, sl]
        ms = jnp.mean(o * o, axis=-1, keepdims=True)
        on = o * lax.rsqrt(ms + RMS_EPS) * glag_ref[...] * _silu(g_ref[:, sl])
        acc = acc + _dot(on.astype(BF16), wout_ref[SSD_WIDTH + h * GLA_DV:SSD_WIDTH + (h + 1) * GLA_DV, :])
    x1_ref[...] = _layer_norm(acc, ln_g_ref[...], ln_b_ref[...])
    _store_states(ssd_state, gla_state, s_states, g_states)


def _scan_common_specs(L, tb, reverse, consts):
    nb = L // tb
    if reverse:
        tok = lambda w: pl.BlockSpec((None, tb, w), lambda b, i: (b, nb - 1 - i, 0))
    else:
        tok = lambda w: pl.BlockSpec((None, tb, w), lambda b, i: (b, i, 0))
    const = lambda a: pl.BlockSpec(a.shape, lambda b, i: (0,) * a.ndim, pipeline_mode=pl.Buffered(1))
    return tok, [const(a) for a in consts]


def _scan_bwd_call(sm, xs, bc, q, k, v, alog, wgk, bgk, tri_s, tri_g, e2):
    B, L, _ = sm.shape
    tb = SCAN_BLOCK
    consts = (alog, wgk, bgk, tri_s, tri_g, e2)
    tok, const_specs = _scan_common_specs(L, tb, True, consts)
    return pl.pallas_call(
        _scan_bwd_kernel,
        grid=(B, L // tb),
        in_specs=[tok(LANES), tok(SSD_WIDTH), tok(BC_WIDTH), tok(GLA_KEY), tok(GLA_KEY), tok(GLA_VAL)] + const_specs,
        out_specs=[tok(SSD_WIDTH), tok(GLA_VAL)],
        out_shape=[jax.ShapeDtypeStruct((B, L, SSD_WIDTH), F32), jax.ShapeDtypeStruct((B, L, GLA_VAL), F32)],
        scratch_shapes=[pltpu.VMEM((SSD_GROUPS, SSD_STATE, SSD_GW), F32),
                        pltpu.VMEM((GLA_HEADS, GLA_DK, GLA_DV), F32)],
        compiler_params=pltpu.CompilerParams(dimension_semantics=("arbitrary", "arbitrary"),
                                             vmem_limit_bytes=VMEM_LIMIT_BYTES),
        name="scan_bwd",
    )(sm, xs, bc, q, k, v, *consts)


def _scan_fwd_call(sm, xs, bc, q, k, v, alog, wgk, bgk, tri_s, tri_g, e2,
                   x, z, g, yb, ob, dskip, ssdg, glag, wout, ln_g, ln_b):
    B, L, _ = sm.shape
    tb = SCAN_BLOCK
    consts_a = (alog, wgk, bgk, tri_s, tri_g, e2)
    consts_b = (dskip, ssdg, glag, wout, ln_g, ln_b)
    tok, const_a = _scan_common_specs(L, tb, False, consts_a)
    _, const_b = _scan_common_specs(L, tb, False, consts_b)
    return pl.pallas_call(
        _scan_fwd_kernel,
        grid=(B, L // tb),
        in_specs=([tok(LANES), tok(SSD_WIDTH), tok(BC_WIDTH), tok(GLA_KEY), tok(GLA_KEY), tok(GLA_VAL)] + const_a
                  + [tok(D_MODEL), tok(SSD_WIDTH), tok(GLA_VAL), tok(SSD_WIDTH), tok(GLA_VAL)] + const_b),
        out_specs=tok(D_MODEL),
        out_shape=jax.ShapeDtypeStruct((B, L, D_MODEL), F32),
        scratch_shapes=[pltpu.VMEM((SSD_GROUPS, SSD_STATE, SSD_GW), F32),
                        pltpu.VMEM((GLA_HEADS, GLA_DK, GLA_DV), F32)],
        compiler_params=pltpu.CompilerParams(dimension_semantics=("arbitrary", "arbitrary"),
                                             vmem_limit_bytes=VMEM_LIMIT_BYTES),
        name="scan_fwd",
    )(sm, xs, bc, q, k, v, *consts_a, x, z, g, yb, ob, *consts_b)


def _memkv_kernel(mem_ref, wk_ref, wv_ref, k_ref, v_ref):
    m = mem_ref[...].astype(BF16)
    k_ref[...] = _dot(m, wk_ref[...]).astype(BF16)
    v_ref[...] = _dot(m, wv_ref[...]).astype(BF16)


def _memkv_call(mem, wk, wv):
    B, M, _ = mem.shape
    blk = pl.BlockSpec((None, M, D_MODEL), lambda b: (b, 0, 0))
    const = lambda a: pl.BlockSpec(a.shape, lambda b: (0,) * a.ndim, pipeline_mode=pl.Buffered(1))
    return pl.pallas_call(
        _memkv_kernel,
        grid=(B,),
        in_specs=[blk, const(wk), const(wv)],
        out_specs=[blk, blk],
        out_shape=[jax.ShapeDtypeStruct((B, M, D_MODEL), BF16)] * 2,
        compiler_params=pltpu.CompilerParams(dimension_semantics=("parallel",),
                                             vmem_limit_bytes=VMEM_LIMIT_BYTES),
        name="memkv",
    )(mem, wk, wv)


def _attn_part(subs, x_ref, mk_ref, mv_ref, wq_ref, wo_ref, ln2g_ref, ln2b_ref):
    x1s = [x_ref[rows, :] for rows in subs]
    qfs = [_dot(x1.astype(BF16), wq_ref[...]) for x1 in x1s]
    accs = [ALPHA * x1 for x1 in x1s]
    for h in range(MEM_HEADS):
        sl = slice(h * MEM_HD, (h + 1) * MEM_HD)
        ss = [_dot_nt(qf[:, sl].astype(BF16), mk_ref[:, sl]) * (MEM_HD ** -0.5) for qf in qfs]
        ps = []
        for s in ss:
            e = jnp.exp(s - jnp.max(s, axis=-1, keepdims=True))
            ps.append((e / jnp.sum(e, axis=-1, keepdims=True)).astype(BF16))
        ohs = [_dot(p, mv_ref[:, sl]).astype(BF16) for p in ps]
        accs = [acc + _dot(oh, wo_ref[sl, :]) for acc, oh in zip(accs, ohs)]
    return [_layer_norm(acc, ln2g_ref[...], ln2b_ref[...]) for acc in accs]


def _mlp_part(subs, x2s, w1_ref, w2_ref, ln3g_ref, ln3b_ref, out_ref):
    x2bs = [x2.astype(BF16) for x2 in x2s]
    accs = [ALPHA * x2 for x2 in x2s]
    for c in range(0, D_FF, FF_CHUNK):
        hs = [jnp.maximum(_dot(x2b, w1_ref[:, c:c + FF_CHUNK]), 0.0) for x2b in x2bs]
        accs = [acc + _dot((hdn * hdn).astype(BF16), w2_ref[c:c + FF_CHUNK, :]) for acc, hdn in zip(accs, hs)]
    for rows, acc in zip(subs, accs):
        out_ref[rows, :] = _layer_norm(acc, ln3g_ref[...], ln3b_ref[...])


def _attn_mlp_kernel(x_ref, mk_ref, mv_ref, wq_ref, wo_ref, ln2g_ref, ln2b_ref, w1_ref, w2_ref,
                     ln3g_ref, ln3b_ref, out_ref):
    subs = [slice(r, r + ATTN_SUB) for r in range(0, x_ref.shape[0], ATTN_SUB)]
    x2s = _attn_part(subs, x_ref, mk_ref, mv_ref, wq_ref, wo_ref, ln2g_ref, ln2b_ref)
    _mlp_part(subs, x2s, w1_ref, w2_ref, ln3g_ref, ln3b_ref, out_ref)


def _attn_mlp_call(x1, mk, mv, wq, wo, ln2g, ln2b, w1, w2, ln3g, ln3b):
    B, L, _ = x1.shape
    M = mk.shape[1]
    tm = ATTN_BLOCK
    tok = pl.BlockSpec((None, tm, D_MODEL), lambda b, i: (b, i, 0))
    memb = pl.BlockSpec((None, M, D_MODEL), lambda b, i: (b, 0, 0))
    const = lambda a: pl.BlockSpec(a.shape, lambda b, i: (0,) * a.ndim, pipeline_mode=pl.Buffered(1))
    return pl.pallas_call(
        _attn_mlp_kernel,
        grid=(B, L // tm),
        in_specs=[tok, memb, memb, const(wq), const(wo), const(ln2g), const(ln2b), const(w1), const(w2),
                  const(ln3g), const(ln3b)],
        out_specs=tok,
        out_shape=jax.ShapeDtypeStruct((B, L, D_MODEL), F32),
        compiler_params=pltpu.CompilerParams(dimension_semantics=("parallel", "parallel"),
                                             vmem_limit_bytes=VMEM_LIMIT_BYTES),
        name="attn_mlp",
    )(x1, mk, mv, wq, wo, ln2g, ln2b, w1, w2, ln3g, ln3b)


def _prepare(w_in, conv_w, conv_b, a_log_f, a_log_b, dt_bias_f, dt_bias_b, d_skip, ssd_norm_g,
             w_gk_f, b_gk_f, w_gk_b, b_gk_b, gla_norm_g, w_out, ln1_g, ln1_b,
             w_mq, w_mk, w_mv, w_mo, ln2_g, ln2_b, w_ff1, w_ff2, ln3_g, ln3_b):
    offs = [0]
    for s in IN_SIZES:
        offs.append(offs[-1] + s)
    col = lambda i: w_in[:, offs[i]:offs[i + 1]]
    wz = col(0).astype(BF16)
    wxbc = jnp.concatenate([col(1), col(2), col(3)], axis=1).astype(BF16)
    wq, wk, wv, wg = (col(i).astype(BF16) for i in (6, 7, 8, 9))
    pad = LANES - 2 * SSD_HEADS - GLA_LOWRANK
    wsm = jnp.concatenate([col(4), col(5), col(10), jnp.zeros((D_MODEL, pad), F32)], axis=1).astype(BF16)
    row = lambda a: a.reshape(1, -1).astype(F32)
    zpad = jnp.zeros((1, LANES - 2 * SSD_HEADS), F32)
    dtbias = jnp.concatenate([row(dt_bias_f), row(dt_bias_b), zpad], axis=1)
    alog = jnp.concatenate([row(a_log_f), row(a_log_b), zpad], axis=1)

    def gk_pad(w):
        return jnp.zeros((LANES, GLA_KEY), F32).at[SM_LR:SM_LR + GLA_LOWRANK].set(w).astype(BF16)

    def tri3(qn, reverse):
        r = jnp.arange(qn)[:, None]
        c = jnp.arange(qn)[None, :]
        t = (c >= r) if reverse else (r >= c)
        return jnp.tile(t.astype(BF16), (1, 3))

    def e2(lane_off):
        r = jnp.arange(LANES)[:, None]
        c = jnp.arange(SSD_WIDTH)[None, :]
        e = (c // SSD_HEADDIM == r - lane_off).astype(BF16)
        return jnp.concatenate([e, e], axis=0)

    return dict(
        proj=(wz, wxbc, wq, wk, wv, wg, wsm, conv_w.astype(F32), row(conv_b), dtbias),
        bwd=(alog, gk_pad(w_gk_b), row(b_gk_b), tri3(SSD_CHUNK, True), tri3(GLA_CHUNK, True), e2(SM_DTB)),
        fwd=(alog, gk_pad(w_gk_f), row(b_gk_f), tri3(SSD_CHUNK, False), tri3(GLA_CHUNK, False), e2(SM_DTF)),
        comb=(row(jnp.repeat(d_skip, SSD_HEADDIM)), row(ssd_norm_g), row(gla_norm_g), w_out.astype(BF16),
              row(ln1_g), row(ln1_b)),
        memkv=(w_mk.astype(BF16), w_mv.astype(BF16)),
        attn=(w_mq.astype(BF16), w_mo.astype(BF16), row(ln2_g), row(ln2_b), w_ff1.astype(BF16),
              w_ff2.astype(BF16), row(ln3_g), row(ln3_b)),
    )


def _encoder_layer(x, mem, p):
    z, xs, bc, q, k, v, g, sm = _proj_call(x, *p["proj"])
    yb, ob = _scan_bwd_call(sm, xs, bc, q, k, v, *p["bwd"])
    x1 = _scan_fwd_call(sm, xs, bc, q, k, v, *p["fwd"], x, z, g, yb, ob, *p["comb"])
    mk, mv = _memkv_call(mem, *p["memkv"])
    return _attn_mlp_call(x1, mk, mv, *p["attn"])


def kernel(x_prompt, x_sample, mem_prompt, mem_sample, w_in, conv_w, conv_b, a_log_f, a_log_b, dt_bias_f,
           dt_bias_b, d_skip, ssd_norm_g, w_gk_f, b_gk_f, w_gk_b, b_gk_b, gla_norm_g, w_out, ln1_g, ln1_b,
           w_mq, w_mk, w_mv, w_mo, ln2_g, ln2_b, w_ff1, w_ff2, ln3_g, ln3_b):
    params = (w_in, conv_w, conv_b, a_log_f, a_log_b, dt_bias_f, dt_bias_b, d_skip, ssd_norm_g,
              w_gk_f, b_gk_f, w_gk_b, b_gk_b, gla_norm_g, w_out, ln1_g, ln1_b,
              w_mq, w_mk, w_mv, w_mo, ln2_g, ln2_b, w_ff1, w_ff2, ln3_g, ln3_b)
    assert all(a.shape[0] == DEPTH for a in params)
    p = _prepare(*[a[0] for a in params])
    return (_encoder_layer(x_prompt, mem_prompt, p), _encoder_layer(x_sample, mem_sample, p))
```

```python
import jax
import jax.numpy as jnp
from jax import lax
from jax.experimental import pallas as pl
from jax.experimental.pallas import tpu as pltpu

F32 = jnp.float32
BF16 = jnp.bfloat16

D_MODEL = 1024
SSD_WIDTH = 1024
SSD_HEADDIM = 64
SSD_HEADS = 16
SSD_GROUPS = 2
SSD_HPG = 8
SSD_STATE = 128
SSD_CHUNK = 128
SSD_GW = SSD_HPG * SSD_HEADDIM
CONV_K = 5
CONV_CH = SSD_WIDTH + 2 * SSD_GROUPS * SSD_STATE
BC_WIDTH = 2 * SSD_GROUPS * SSD_STATE
GLA_HEADS = 4
GLA_KEY = 512
GLA_VAL = 1024
GLA_DK = 128
GLA_DV = 256
GLA_LOWRANK = 16
GLA_GATE_NORM = 16.0
GLA_CHUNK = 64
MIX_WIDTH = SSD_WIDTH + GLA_VAL
IN_SIZES = (SSD_WIDTH, SSD_WIDTH, SSD_GROUPS * SSD_STATE, SSD_GROUPS * SSD_STATE, SSD_HEADS, SSD_HEADS,
            GLA_KEY, GLA_KEY, GLA_VAL, GLA_VAL, GLA_LOWRANK)
MEM_HEADS = 4
MEM_HD = 256
D_FF = 4096
DEPTH = 1
ALPHA = (2 * DEPTH) ** 0.25
LN_EPS = 1e-5
RMS_EPS = 1e-5
LOG2E = 1.4426950408889634

LANES = 128
SUBLANES = 8
VMEM_LIMIT_BYTES = 56 * 1024 * 1024

SM_DTF = 0
SM_DTB = SSD_HEADS
SM_LR = 2 * SSD_HEADS
SM_USED_DT = 2 * SSD_HEADS

HALO = SUBLANES
CONV_L = (CONV_K - 1) // 2

PROJ_BLOCK = 512
CONV_COLS = 512
SCAN_BLOCK = 512
ATTN_BLOCK = 1024
ATTN_SUB = 256
FF_CHUNK = 1024


def _dot(a, b):
    return jnp.dot(a, b, preferred_element_type=F32)


def _dot_nt(a, b):
    return lax.dot_general(a, b, (((1,), (1,)), ((), ())), preferred_element_type=F32)


def _dot_tn(a, b):
    return lax.dot_general(a, b, (((0,), (0,)), ((), ())), preferred_element_type=F32)


def _sigmoid(x):
    return 1.0 / (1.0 + jnp.exp(-x))


def _silu(x):
    return x * _sigmoid(x)


def _softplus(x):
    return jnp.maximum(x, 0.0) + jnp.log1p(jnp.exp(-jnp.abs(x)))


def _split3(a):
    hi = a.astype(BF16)
    r = a - hi.astype(F32)
    mid = r.astype(BF16)
    lo = (r - mid.astype(F32)).astype(BF16)
    return hi, mid, lo


def _cumsum_rows(tri3, a):
    hi, mid, lo = _split3(a)
    return _dot(tri3, jnp.concatenate([hi, mid, lo], axis=0))


def _expand_heads(a, e2):
    hi = a.astype(BF16)
    lo = (a - hi.astype(F32)).astype(BF16)
    return _dot(jnp.concatenate([hi, lo], axis=1), e2)


def _layer_norm(h, g, b):
    mu = jnp.mean(h, axis=-1, keepdims=True)
    d = h - mu
    var = jnp.mean(d * d, axis=-1, keepdims=True)
    return d * lax.rsqrt(var + LN_EPS) * g + b


def _proj_kernel(x_ref, xp_ref, xn_ref, wz_ref, wxbc_ref, wq_ref, wk_ref, wv_ref, wg_ref, wsm_ref,
                 convw_ref, convb_ref, dtbias_ref,
                 z_ref, xs_ref, bc_ref, q_ref, k_ref, v_ref, g_ref, sm_ref, ext_ref):
    i = pl.program_id(1)
    nb = pl.num_programs(1)
    tm = x_ref.shape[0]
    xb = x_ref[...].astype(BF16)
    has_prev = i > 0
    has_next = i < nb - 1
    x_ext = jnp.concatenate([xp_ref[...], x_ref[...], xn_ref[...]], axis=0).astype(BF16)
    row = lax.broadcasted_iota(jnp.int32, (tm + 2 * HALO, 1), 0)
    keep = jnp.logical_and(jnp.logical_or(row >= HALO, has_prev), jnp.logical_or(row < HALO + tm, has_next))
    cw = CONV_COLS
    for c in range(0, CONV_CH, cw):
        u = jnp.where(keep, _dot(x_ext, wxbc_ref[:, c:c + cw]), 0.0)
        for t in range(cw // LANES):
            ext_ref[c // LANES + t, :, :] = u[:, t * LANES:(t + 1) * LANES]

    def conv_tile(ct):
        ls = slice(ct * LANES, (ct + 1) * LANES)
        acc = jnp.broadcast_to(convb_ref[:, ls], (tm, LANES))
        for j in range(CONV_K):
            r0 = HALO - CONV_L + j
            acc = acc + convw_ref[j:j + 1, ls] * ext_ref[ct, r0:r0 + tm, :]
        return _silu(acc)

    n_xs = SSD_WIDTH // LANES
    for ct in range(n_xs):
        xs_ref[:, ct * LANES:(ct + 1) * LANES] = conv_tile(ct)
    for ct in range(n_xs, CONV_CH // LANES):
        bc_ref[:, (ct - n_xs) * LANES:(ct - n_xs + 1) * LANES] = conv_tile(ct).astype(BF16)

    z_ref[...] = _dot(xb, wz_ref[...]).astype(BF16)
    q_ref[...] = _dot(xb, wq_ref[...]) * (GLA_DK ** -0.5)
    k_ref[...] = _dot(xb, wk_ref[...])
    v_ref[...] = _dot(xb, wv_ref[...]).astype(BF16)
    g_ref[...] = _dot(xb, wg_ref[...]).astype(BF16)
    sm = _dot(xb, wsm_ref[...])
    lane = lax.broadcasted_iota(jnp.int32, sm.shape, 1)
    sm_ref[...] = jnp.where(lane < SM_USED_DT, _softplus(sm + dtbias_ref[...]), sm)


def _proj_call(x, wz, wxbc, wq, wk, wv, wg, wsm, convw, convb, dtbias):
    B, L, _ = x.shape
    tm = PROJ_BLOCK
    nb = L // tm
    hb = tm // HALO
    const = lambda shape: pl.BlockSpec(shape, lambda b, i: (0,) * len(shape), pipeline_mode=pl.Buffered(1))
    tok = lambda w: pl.BlockSpec((None, tm, w), lambda b, i: (b, i, 0))
    outs = ((SSD_WIDTH, BF16), (SSD_WIDTH, F32), (BC_WIDTH, BF16), (GLA_KEY, F32), (GLA_KEY, F32), (GLA_VAL, BF16),
            (GLA_VAL, BF16), (LANES, F32))
    return pl.pallas_call(
        _proj_kernel,
        grid=(B, nb),
        in_specs=[
            tok(D_MODEL),
            pl.BlockSpec((None, HALO, D_MODEL), lambda b, i: (b, jnp.maximum(i * hb - 1, 0), 0)),
            pl.BlockSpec((None, HALO, D_MODEL), lambda b, i: (b, jnp.minimum((i + 1) * hb, L // HALO - 1), 0)),
            const(wz.shape), const(wxbc.shape), const(wq.shape), const(wk.shape), const(wv.shape),
            const(wg.shape), const(wsm.shape), const(convw.shape), const(convb.shape), const(dtbias.shape),
        ],
        out_specs=[tok(w) for w, _ in outs],
        out_shape=[jax.ShapeDtypeStruct((B, L, w), dt) for w, dt in outs],
        scratch_shapes=[pltpu.VMEM((CONV_CH // LANES, tm + 2 * HALO, LANES), F32)],
        compiler_params=pltpu.CompilerParams(dimension_semantics=("parallel", "parallel"),
                                             vmem_limit_bytes=VMEM_LIMIT_BYTES),
        name="proj",
    )(x, x, x, wz, wxbc, wq, wk, wv, wg, wsm, convw, convb, dtbias)


def _ssd_prep(reverse, sm_ref, xs_ref, alog_ref, tri_ref, e2_ref):
    q = SSD_CHUNK
    tb = sm_ref.shape[0]
    sm = sm_ref[...]
    lane1 = lax.broadcasted_iota(jnp.int32, (1, LANES), 1)
    a_neg = jnp.where(lane1 < SM_USED_DT, -jnp.exp(alog_ref[...]) * LOG2E, 0.0)
    hi, mid, lo = _split3(sm * a_neg)
    tri = tri_ref[...]
    acs_c, last_c = [], []
    for r0 in range(0, tb, q):
        rows = slice(r0, r0 + q)
        a = _dot(tri, jnp.concatenate([hi[rows], mid[rows], lo[rows]], axis=0))
        acs_c.append(a)
        last_c.append(jnp.broadcast_to(a[0:1, :] if reverse else a[q - 1:q, :], (q, LANES)))
    acs = jnp.concatenate(acs_c, axis=0)
    dte = jnp.exp2(jnp.concatenate(last_c, axis=0) - acs)
    e2 = e2_ref[...]
    dt_x = _expand_heads(sm, e2)
    dtd_x = _expand_heads(sm * dte, e2)
    eacs_x = _expand_heads(jnp.exp2(acs), e2)
    xs = xs_ref[...]
    xdt = (xs * dt_x).astype(BF16)
    lane_w = lax.broadcasted_iota(jnp.int32, (tb, SSD_WIDTH), 1)
    even = (lane_w & SSD_HEADDIM) == 0
    zero = jnp.zeros_like(xdt)
    xdt_e = jnp.where(even, xdt, zero)
    xdt_o = jnp.where(even, zero, xdt)
    xdd = (xs * dtd_x).astype(BF16)
    preps = []
    for c, r0 in enumerate(range(0, tb, q)):
        rows = slice(r0, r0 + q)
        ex = eacs_x[rows]
        cdec_x = ex[0:1, :] if reverse else ex[q - 1:q, :]
        preps.append(dict(acs=acs_c[c], acs_t=acs_c[c].T, xdt_e=xdt_e[rows], xdt_o=xdt_o[rows], xdd=xdd[rows],
                          eacs_x=ex, cdec_x=cdec_x))
    return preps


def _ssd_intra(reverse, r0, p, bc_ref):
    q = SSD_CHUNK
    rows = slice(r0, r0 + q)
    lane_off = SM_DTB if reverse else SM_DTF
    li = lax.broadcasted_iota(jnp.int32, (q, q), 0)
    si = lax.broadcasted_iota(jnp.int32, (q, q), 1)
    mask = (si >= li) if reverse else (li >= si)
    acs, acs_t = p["acs"], p["acs_t"]
    ys, cgs, snews = [], [], []
    for g in range(SSD_GROUPS):
        b0 = g * SSD_STATE
        c0 = SSD_GROUPS * SSD_STATE + g * SSD_STATE
        bg = bc_ref[rows, b0:b0 + SSD_STATE]
        cg = bc_ref[rows, c0:c0 + SSD_STATE]
        cb = _dot_nt(cg, bg)
        for pr in range(SSD_GW // LANES):
            pl0 = g * SSD_GW + pr * LANES
            ms = []
            for hh in range(LANES // SSD_HEADDIM):
                col = lane_off + g * SSD_HPG + pr * (LANES // SSD_HEADDIM) + hh
                seg = acs[:, col:col + 1] - acs_t[col:col + 1, :]
                lm = jnp.exp2(jnp.where(mask, seg, -jnp.inf))
                ms.append((cb * lm).astype(BF16))
            rhs = jnp.concatenate([p["xdt_e"][:, pl0:pl0 + LANES], p["xdt_o"][:, pl0:pl0 + LANES]], axis=0)
            ys.append(_dot(jnp.concatenate(ms, axis=1), rhs))
        gl = slice(g * SSD_GW, (g + 1) * SSD_GW)
        cgs.append(cg)
        snews.append(_dot_tn(bg, p["xdd"][:, gl]))
    return ys, cgs, snews


def _ssd_carry(p, ys, cgs, snews, states):
    outs, new_states = [], []
    for g in range(SSD_GROUPS):
        gl = slice(g * SSD_GW, (g + 1) * SSD_GW)
        s_in = states[g]
        y_off = _dot(cgs[g], s_in.astype(BF16)) * p["eacs_x"][:, gl]
        npr = SSD_GW // LANES
        for pr in range(npr):
            outs.append(ys[g * npr + pr] + y_off[:, pr * LANES:(pr + 1) * LANES])
        new_states.append(s_in * p["cdec_x"][:, gl] + snews[g])
    return jnp.concatenate(outs, axis=1), new_states


def _gla_prep(reverse, sm_ref, q_ref, k_ref, wgk_ref, bgk_ref, tri_ref):
    tb = sm_ref.shape[0]
    q = GLA_CHUNK
    u = _dot(sm_ref[...].astype(BF16), wgk_ref[...]) + bgk_ref[...]
    lg = (jnp.minimum(u, 0.0) * LOG2E - jnp.log2(1.0 + jnp.exp2(-jnp.abs(u) * LOG2E))) / GLA_GATE_NORM
    hi, mid, lo = _split3(lg)
    tri = tri_ref[...]
    bcs_c, last_c = [], []
    for r0 in range(0, tb, q):
        rows = slice(r0, r0 + q)
        b = _dot(tri, jnp.concatenate([hi[rows], mid[rows], lo[rows]], axis=0))
        bcs_c.append(b)
        last_c.append(b[0:1, :] if reverse else b[q - 1:q, :])
    nc = len(bcs_c)
    first = lambda c: (c % 2 == 1) if reverse else (c % 2 == 0)
    other = lambda c: c + 1 if c % 2 == 0 else c - 1
    zero_row = jnp.zeros((q, GLA_KEY), F32)
    bcs = jnp.concatenate(bcs_c, axis=0)
    last_b = jnp.concatenate([jnp.broadcast_to(l, (q, GLA_KEY)) for l in last_c], axis=0)
    q_shift = jnp.concatenate([zero_row if first(c) else jnp.broadcast_to(last_c[other(c)], (q, GLA_KEY))
                               for c in range(nc)], axis=0)
    k_shift = jnp.concatenate([jnp.broadcast_to(last_c[other(c)], (q, GLA_KEY)) if first(c) else zero_row
                               for c in range(nc)], axis=0)
    kk = k_ref[...]
    qq = q_ref[...]
    qt = (qq * jnp.exp2(bcs)).astype(BF16)
    qs = (qq * jnp.exp2(bcs + q_shift)).astype(BF16)
    kt = (kk * jnp.exp2(-bcs)).astype(BF16)
    ke = (kk * jnp.exp2(last_b - bcs)).astype(BF16)
    ks = (kk * jnp.exp2(last_b - bcs + k_shift)).astype(BF16)
    dec_cols = [[jnp.broadcast_to(jnp.exp2((last_c[2 * p] + last_c[2 * p + 1])[:, h * GLA_DK:(h + 1) * GLA_DK]),
                                  (GLA_DK, GLA_DK)).T for h in range(GLA_HEADS)] for p in range(nc // 2)]
    return qt, qs, kt, ke, ks, dec_cols


def _gla_intra(reverse, r0, qt, kt, ke, ks, v_ref):
    q = GLA_CHUNK
    rows = slice(r0, r0 + 2 * q)
    li = lax.broadcasted_iota(jnp.int32, (2 * q, 2 * q), 0)
    si = lax.broadcasted_iota(jnp.int32, (2 * q, 2 * q), 1)
    l_hi, s_hi = li >= q, si >= q
    diag = jnp.logical_and(l_hi == s_hi, (si >= li) if reverse else (li >= si))
    cross = jnp.logical_and(jnp.logical_not(l_hi), s_hi) if reverse else jnp.logical_and(l_hi, jnp.logical_not(s_hi))
    atts, ups = [], []
    for h in range(GLA_HEADS):
        kl = slice(h * GLA_DK, (h + 1) * GLA_DK)
        vl = slice(h * GLA_DV, (h + 1) * GLA_DV)
        p = _dot_nt(qt[rows, kl], jnp.concatenate([kt[rows, kl], ke[rows, kl]], axis=0))
        att = jnp.where(diag, p[:, :2 * q], jnp.where(cross, p[:, 2 * q:], 0.0))
        atts.append(att.astype(BF16))
        ups.append(_dot_tn(ks[rows, kl], v_ref[rows, vl]))
    return atts, ups


def _gla_carry(r0, qs, dec_cols, atts, ups, v_ref, states):
    rows = slice(r0, r0 + 2 * GLA_CHUNK)
    outs, new_states = [], []
    for h in range(GLA_HEADS):
        kl = slice(h * GLA_DK, (h + 1) * GLA_DK)
        vl = slice(h * GLA_DV, (h + 1) * GLA_DV)
        st = states[h]
        lhs = jnp.concatenate([qs[rows, kl], atts[h]], axis=1)
        rhs = jnp.concatenate([st.astype(BF16), v_ref[rows, vl]], axis=0)
        outs.append(_dot(lhs, rhs))
        dcol = dec_cols[h]
        new_states.append(st * jnp.concatenate([dcol, dcol], axis=1) + ups[h])
    return jnp.concatenate(outs, axis=1), new_states


def _run_scans(reverse, sm_ref, xs_ref, bc_ref, q_ref, k_ref, v_ref, alog_ref, wgk_ref, bgk_ref,
               tri_s_ref, tri_g_ref, e2_ref, ssd_state, gla_state):
    tb = sm_ref.shape[0]

    @pl.when(pl.program_id(1) == 0)
    def _():
        ssd_state[...] = jnp.zeros_like(ssd_state)
        gla_state[...] = jnp.zeros_like(gla_state)

    s_states = [ssd_state[g] for g in range(SSD_GROUPS)]
    g_states = [gla_state[h] for h in range(GLA_HEADS)]
    s_chunks = list(range(0, tb, SSD_CHUNK))
    g_chunks = list(range(0, tb, 2 * GLA_CHUNK))
    preps = _ssd_prep(reverse, sm_ref, xs_ref, alog_ref, tri_s_ref, e2_ref)
    qt, qs, kt, ke, ks, dec_cols = _gla_prep(reverse, sm_ref, q_ref, k_ref, wgk_ref, bgk_ref, tri_g_ref)
    s_intra = [_ssd_intra(reverse, r0, p, bc_ref) for r0, p in zip(s_chunks, preps)]
    g_intra = [_gla_intra(reverse, r0, qt, kt, ke, ks, v_ref) for r0 in g_chunks]
    s_order = range(len(s_chunks) - 1, -1, -1) if reverse else range(len(s_chunks))
    g_order = range(len(g_chunks) - 1, -1, -1) if reverse else range(len(g_chunks))
    y_c = [None] * len(s_chunks)
    o_c = [None] * len(g_chunks)
    for c in s_order:
        y_c[c], s_states = _ssd_carry(preps[c], *s_intra[c], s_states)
    for c in g_order:
        o_c[c], g_states = _gla_carry(g_chunks[c], qs, dec_cols[c], *g_intra[c], v_ref, g_states)
    return jnp.concatenate(y_c, axis=0), jnp.concatenate(o_c, axis=0), s_states, g_states


def _store_states(ssd_state, gla_state, s_states, g_states):
    for g in range(SSD_GROUPS):
        ssd_state[g] = s_states[g]
    for h in range(GLA_HEADS):
        gla_state[h] = g_states[h]


def _scan_bwd_kernel(sm_ref, xs_ref, bc_ref, q_ref, k_ref, v_ref, alog_ref, wgk_ref, bgk_ref,
                     tri_s_ref, tri_g_ref, e2_ref, y_ref, o_ref, ssd_state, gla_state):
    y, o, s_states, g_states = _run_scans(True, sm_ref, xs_ref, bc_ref, q_ref, k_ref, v_ref, alog_ref, wgk_ref,
                                          bgk_ref, tri_s_ref, tri_g_ref, e2_ref, ssd_state, gla_state)
    y_ref[...] = y.astype(BF16)
    o_ref[...] = o.astype(BF16)
    _store_states(ssd_state, gla_state, s_states, g_states)


def _scan_fwd_kernel(sm_ref, xs_ref, bc_ref, q_ref, k_ref, v_ref, alog_ref, wgk_ref, bgk_ref,
                     tri_s_ref, tri_g_ref, e2_ref,
                     x_ref, z_ref, g_ref, yb_ref, ob_ref, dskip_ref, ssdg_ref, glag_ref, wout_ref,
                     ln_g_ref, ln_b_ref,
                     x1_ref, ssd_state, gla_state):
    yf, of, s_states, g_states = _run_scans(False, sm_ref, xs_ref, bc_ref, q_ref, k_ref, v_ref, alog_ref, wgk_ref,
                                            bgk_ref, tri_s_ref, tri_g_ref, e2_ref, ssd_state, gla_state)
    acc = ALPHA * x_ref[...]
    gw = SSD_WIDTH // SSD_GROUPS
    for g in range(SSD_GROUPS):
        sl = slice(g * gw, (g + 1) * gw)
        y = yf[:, sl] + yb_ref[:, sl].astype(F32) + dskip_ref[:, sl] * xs_ref[:, sl]
        y = y * _silu(z_ref[:, sl].astype(F32))
        ms = jnp.mean(y * y, axis=-1, keepdims=True)
        yn = y * lax.rsqrt(ms + RMS_EPS) * ssdg_ref[:, sl]
        acc = acc + _dot(yn.astype(BF16), wout_ref[sl, :])
    for h in range(GLA_HEADS):
        sl = slice(h * GLA_DV, (h + 1) * GLA_DV)
        o = of[:, sl] + ob_ref[:, sl].astype(F32)
        ms = jnp.mean(o * o, axis=-1, keepdims=True)
        on = o * lax.rsqrt(ms + RMS_EPS) * glag_ref[...] * _silu(g_ref[:, sl].astype(F32))
        acc = acc + _dot(on.astype(BF16), wout_ref[SSD_WIDTH + h * GLA_DV:SSD_WIDTH + (h + 1) * GLA_DV, :])
    x1_ref[...] = _layer_norm(acc, ln_g_ref[...], ln_b_ref[...])
    _store_states(ssd_state, gla_state, s_states, g_states)


def _scan_common_specs(L, tb, reverse, consts):
    nb = L // tb
    if reverse:
        tok = lambda w: pl.BlockSpec((None, tb, w), lambda b, i: (b, nb - 1 - i, 0))
    else:
        tok = lambda w: pl.BlockSpec((None, tb, w), lambda b, i: (b, i, 0))
    const = lambda a: pl.BlockSpec(a.shape, lambda b, i: (0,) * a.ndim, pipeline_mode=pl.Buffered(1))
    return tok, [const(a) for a in consts]


def _scan_bwd_call(sm, xs, bc, q, k, v, alog, wgk, bgk, tri_s, tri_g, e2):
    B, L, _ = sm.shape
    tb = SCAN_BLOCK
    consts = (alog, wgk, bgk, tri_s, tri_g, e2)
    tok, const_specs = _scan_common_specs(L, tb, True, consts)
    return pl.pallas_call(
        _scan_bwd_kernel,
        grid=(B, L // tb),
        in_specs=[tok(LANES), tok(SSD_WIDTH), tok(BC_WIDTH), tok(GLA_KEY), tok(GLA_KEY), tok(GLA_VAL)] + const_specs,
        out_specs=[tok(SSD_WIDTH), tok(GLA_VAL)],
        out_shape=[jax.ShapeDtypeStruct((B, L, SSD_WIDTH), BF16), jax.ShapeDtypeStruct((B, L, GLA_VAL), BF16)],
        scratch_shapes=[pltpu.VMEM((SSD_GROUPS, SSD_STATE, SSD_GW), F32),
                        pltpu.VMEM((GLA_HEADS, GLA_DK, GLA_DV), F32)],
        compiler_params=pltpu.CompilerParams(dimension_semantics=("arbitrary", "arbitrary"),
                                             vmem_limit_bytes=VMEM_LIMIT_BYTES),
        name="scan_bwd",
    )(sm, xs, bc, q, k, v, *consts)


def _scan_fwd_call(sm, xs, bc, q, k, v, alog, wgk, bgk, tri_s, tri_g, e2,
                   x, z, g, yb, ob, dskip, ssdg, glag, wout, ln_g, ln_b):
    B, L, _ = sm.shape
    tb = SCAN_BLOCK
    consts_a = (alog, wgk, bgk, tri_s, tri_g, e2)
    consts_b = (dskip, ssdg, glag, wout, ln_g, ln_b)
    tok, const_a = _scan_common_specs(L, tb, False, consts_a)
    _, const_b = _scan_common_specs(L, tb, False, consts_b)
    return pl.pallas_call(
        _scan_fwd_kernel,
        grid=(B, L // tb),
        in_specs=([tok(LANES), tok(SSD_WIDTH), tok(BC_WIDTH), tok(GLA_KEY), tok(GLA_KEY), tok(GLA_VAL)] + const_a
                  + [tok(D_MODEL), tok(SSD_WIDTH), tok(GLA_VAL), tok(SSD_WIDTH), tok(GLA_VAL)] + const_b),
        out_specs=tok(D_MODEL),
        out_shape=jax.ShapeDtypeStruct((B, L, D_MODEL), F32),
        scratch_shapes=[pltpu.VMEM((SSD_GROUPS, SSD_STATE, SSD_GW), F32),
                        pltpu.VMEM((GLA_HEADS, GLA_DK, GLA_DV), F32)],
        compiler_params=pltpu.CompilerParams(dimension_semantics=("arbitrary", "arbitrary"),
                                             vmem_limit_bytes=VMEM_LIMIT_BYTES),
        name="scan_fwd",
    )(sm, xs, bc, q, k, v, *consts_a, x, z, g, yb, ob, *consts_b)


def _memkv_kernel(mem_ref, wk_ref, wv_ref, k_ref, v_ref):
    m = mem_ref[...].astype(BF16)
    k_ref[...] = _dot(m, wk_ref[...]).astype(BF16)
    v_ref[...] = _dot(m, wv_ref[...]).astype(BF16)


def _memkv_call(mem, wk, wv):
    B, M, _ = mem.shape
    blk = pl.BlockSpec((None, M, D_MODEL), lambda b: (b, 0, 0))
    const = lambda a: pl.BlockSpec(a.shape, lambda b: (0,) * a.ndim, pipeline_mode=pl.Buffered(1))
    return pl.pallas_call(
        _memkv_kernel,
        grid=(B,),
        in_specs=[blk, const(wk), const(wv)],
        out_specs=[blk, blk],
        out_shape=[jax.ShapeDtypeStruct((B, M, D_MODEL), BF16)] * 2,
        compiler_params=pltpu.CompilerParams(dimension_semantics=("parallel",),
                                             vmem_limit_bytes=VMEM_LIMIT_BYTES),
        name="memkv",
    )(mem, wk, wv)


def _attn_part(subs, x_ref, mk_ref, mv_ref, wq_ref, wo_ref, ln2g_ref, ln2b_ref):
    x1s = [x_ref[rows, :] for rows in subs]
    qfs = [_dot(x1.astype(BF16), wq_ref[...]) for x1 in x1s]
    accs = [ALPHA * x1 for x1 in x1s]
    for h in range(MEM_HEADS):
        sl = slice(h * MEM_HD, (h + 1) * MEM_HD)
        ss = [_dot_nt(qf[:, sl].astype(BF16), mk_ref[:, sl]) * (MEM_HD ** -0.5) for qf in qfs]
        ps = []
        for s in ss:
            e = jnp.exp(s - jnp.max(s, axis=-1, keepdims=True))
            ps.append((e / jnp.sum(e, axis=-1, keepdims=True)).astype(BF16))
        ohs = [_dot(p, mv_ref[:, sl]).astype(BF16) for p in ps]
        accs = [acc + _dot(oh, wo_ref[sl, :]) for acc, oh in zip(accs, ohs)]
    return [_layer_norm(acc, ln2g_ref[...], ln2b_ref[...]) for acc in accs]


def _mlp_part(subs, x2s, w1_ref, w2_ref, ln3g_ref, ln3b_ref, out_ref):
    x2bs = [x2.astype(BF16) for x2 in x2s]
    accs = [ALPHA * x2 for x2 in x2s]
    for c in range(0, D_FF, FF_CHUNK):
        hs = [jnp.maximum(_dot(x2b, w1_ref[:, c:c + FF_CHUNK]), 0.0) for x2b in x2bs]
        accs = [acc + _dot((hdn * hdn).astype(BF16), w2_ref[c:c + FF_CHUNK, :]) for acc, hdn in zip(accs, hs)]
    for rows, acc in zip(subs, accs):
        out_ref[rows, :] = _layer_norm(acc, ln3g_ref[...], ln3b_ref[...])


def _attn_mlp_kernel(x_ref, mk_ref, mv_ref, wq_ref, wo_ref, ln2g_ref, ln2b_ref, w1_ref, w2_ref,
                     ln3g_ref, ln3b_ref, out_ref):
    subs = [slice(r, r + ATTN_SUB) for r in range(0, x_ref.shape[0], ATTN_SUB)]
    x2s = _attn_part(subs, x_ref, mk_ref, mv_ref, wq_ref, wo_ref, ln2g_ref, ln2b_ref)
    _mlp_part(subs, x2s, w1_ref, w2_ref, ln3g_ref, ln3b_ref, out_ref)


def _attn_mlp_call(x1, mk, mv, wq, wo, ln2g, ln2b, w1, w2, ln3g, ln3b):
    B, L, _ = x1.shape
    M = mk.shape[1]
    tm = ATTN_BLOCK
    tok = pl.BlockSpec((None, tm, D_MODEL), lambda b, i: (b, i, 0))
    memb = pl.BlockSpec((None, M, D_MODEL), lambda b, i: (b, 0, 0))
    const = lambda a: pl.BlockSpec(a.shape, lambda b, i: (0,) * a.ndim, pipeline_mode=pl.Buffered(1))
    return pl.pallas_call(
        _attn_mlp_kernel,
        grid=(B, L // tm),
        in_specs=[tok, memb, memb, const(wq), const(wo), const(ln2g), const(ln2b), const(w1), const(w2),
                  const(ln3g), const(ln3b)],
        out_specs=tok,
        out_shape=jax.ShapeDtypeStruct((B, L, D_MODEL), F32),
        compiler_params=pltpu.CompilerParams(dimension_semantics=("parallel", "parallel"),
                                             vmem_limit_bytes=VMEM_LIMIT_BYTES),
        name="attn_mlp",
    )(x1, mk, mv, wq, wo, ln2g, ln2b, w1, w2, ln3g, ln3b)


def _prepare(w_in, conv_w, conv_b, a_log_f, a_log_b, dt_bias_f, dt_bias_b, d_skip, ssd_norm_g,
             w_gk_f, b_gk_f, w_gk_b, b_gk_b, gla_norm_g, w_out, ln1_g, ln1_b,
             w_mq, w_mk, w_mv, w_mo, ln2_g, ln2_b, w_ff1, w_ff2, ln3_g, ln3_b):
    offs = [0]
    for s in IN_SIZES:
        offs.append(offs[-1] + s)
    col = lambda i: w_in[:, offs[i]:offs[i + 1]]
    wz = col(0).astype(BF16)
    wxbc = jnp.concatenate([col(1), col(2), col(3)], axis=1).astype(BF16)
    wq, wk, wv, wg = (col(i).astype(BF16) for i in (6, 7, 8, 9))
    pad = LANES - 2 * SSD_HEADS - GLA_LOWRANK
    wsm = jnp.concatenate([col(4), col(5), col(10), jnp.zeros((D_MODEL, pad), F32)], axis=1).astype(BF16)
    row = lambda a: a.reshape(1, -1).astype(F32)
    zpad = jnp.zeros((1, LANES - 2 * SSD_HEADS), F32)
    dtbias = jnp.concatenate([row(dt_bias_f), row(dt_bias_b), zpad], axis=1)
    alog = jnp.concatenate([row(a_log_f), row(a_log_b), zpad], axis=1)

    def gk_pad(w):
        return jnp.zeros((LANES, GLA_KEY), F32).at[SM_LR:SM_LR + GLA_LOWRANK].set(w).astype(BF16)

    def tri3(qn, reverse):
        r = jnp.arange(qn)[:, None]
        c = jnp.arange(qn)[None, :]
        t = (c >= r) if reverse else (r >= c)
        return jnp.tile(t.astype(BF16), (1, 3))

    def e2(lane_off):
        r = jnp.arange(LANES)[:, None]
        c = jnp.arange(SSD_WIDTH)[None, :]
        e = (c // SSD_HEADDIM == r - lane_off).astype(BF16)
        return jnp.concatenate([e, e], axis=0)

    return dict(
        proj=(wz, wxbc, wq, wk, wv, wg, wsm, conv_w.astype(F32), row(conv_b), dtbias),
        bwd=(alog, gk_pad(w_gk_b), row(b_gk_b), tri3(SSD_CHUNK, True), tri3(GLA_CHUNK, True), e2(SM_DTB)),
        fwd=(alog, gk_pad(w_gk_f), row(b_gk_f), tri3(SSD_CHUNK, False), tri3(GLA_CHUNK, False), e2(SM_DTF)),
        comb=(row(jnp.repeat(d_skip, SSD_HEADDIM)), row(ssd_norm_g), row(gla_norm_g), w_out.astype(BF16),
              row(ln1_g), row(ln1_b)),
        memkv=(w_mk.astype(BF16), w_mv.astype(BF16)),
        attn=(w_mq.astype(BF16), w_mo.astype(BF16), row(ln2_g), row(ln2_b), w_ff1.astype(BF16),
              w_ff2.astype(BF16), row(ln3_g), row(ln3_b)),
    )


def _encoder_layer(x, mem, p):
    z, xs, bc, q, k, v, g, sm = _proj_call(x, *p["proj"])
    yb, ob = _scan_bwd_call(sm, xs, bc, q, k, v, *p["bwd"])
    x1 = _scan_fwd_call(sm, xs, bc, q, k, v, *p["fwd"], x, z, g, yb, ob, *p["comb"])
    mk, mv = _memkv_call(mem, *p["memkv"])
    return _attn_mlp_call(x1, mk, mv, *p["attn"])


def kernel(x_prompt, x_sample, mem_prompt, mem_sample, w_in, conv_w, conv_b, a_log_f, a_log_b, dt_bias_f,
           dt_bias_b, d_skip, ssd_norm_g, w_gk_f, b_gk_f, w_gk_b, b_gk_b, gla_norm_g, w_out, ln1_g, ln1_b,
           w_mq, w_mk, w_mv, w_mo, ln2_g, ln2_b, w_ff1, w_ff2, ln3_g, ln3_b):
    params = (w_in, conv_w, conv_b, a_log_f, a_log_b, dt_bias_f, dt_bias_b, d_skip, ssd_norm_g,
              w_gk_f, b_gk_f, w_gk_b, b_gk_b, gla_norm_g, w_out, ln1_g, ln1_b,
              w_mq, w_mk, w_mv, w_mo, ln2_g, ln2_b, w_ff1, w_ff2, ln3_g, ln3_b)
    assert all(a.shape[0] == DEPTH for a in params)
    p = _prepare(*[a[0] for a in params])
    return (_encoder_layer(x_prompt, mem_prompt, p), _encoder_layer(x_sample, mem_sample, p))
```
